```python
import math
import jax, jax.numpy as jnp
from jax import lax
import numpy as np

D_MODEL = 2048
BATCH = 2
SEQ = 8192
DEPTH = 4

CHUNK = 64
N_MEM = 256
N_BRANCH = 3
D_CONV = D_MODEL // 2
CONV_WIDTH = 31
ATTN_HEAD_DIM = 64
D_ATTN = D_MODEL // 2
N_ATTN_HEADS = D_ATTN // ATTN_HEAD_DIM
LEFT_CHUNKS = 8
BAND = LEFT_CHUNKS + 1
MAX_REL = 256
N_MEM_HEADS = 4
D_MEM = D_MODEL // 2
MEM_HEAD_DIM = D_MEM // N_MEM_HEADS
D_FF = ((8 * D_MODEL // 3 + 255) // 256) * 256
FFN_CONV_WIDTH = 3
D_IN = 2 * D_CONV + 3 * D_ATTN + D_MEM + N_BRANCH * D_MODEL
SPLITS = (2 * D_CONV, 2 * D_CONV + 3 * D_ATTN, 2 * D_CONV + 3 * D_ATTN + D_MEM)
EPS = 1e-6
NEG_INF = -1e30

kernel_name = 'hybrid_gated_conv_chunkattn_mem_encoder'


def rmsnorm(x, g):
    xf = x.astype(jnp.float32)
    y = xf * lax.rsqrt(jnp.mean(xf * xf, axis=-1, keepdims=True) + EPS)
    return (y * g.astype(jnp.float32)).astype(x.dtype)


def layernorm(x, g, b):
    xf = x.astype(jnp.float32)
    mu = jnp.mean(xf, axis=-1, keepdims=True)
    var = jnp.mean(jnp.square(xf - mu), axis=-1, keepdims=True)
    y = (xf - mu) * lax.rsqrt(var + EPS)
    return (y * g.astype(jnp.float32) + b.astype(jnp.float32)).astype(x.dtype)


def causal_dwconv(x, w, b):
    width = w.shape[0]
    ch = x.shape[-1]
    y = lax.conv_general_dilated(
        x, w[:, None, :].astype(x.dtype), window_strides=(1,),
        padding=[(width - 1, 0)], dimension_numbers=('NWC', 'WIO', 'NWC'),
        feature_group_count=ch)
    return y + b.astype(x.dtype)


def conv_module(u, conv_w, conv_b, ln_g, ln_b):
    a, gt = jnp.split(u, 2, axis=-1)
    h = a * jax.nn.sigmoid(gt)
    h = causal_dwconv(h, conv_w, conv_b)
    return jax.nn.silu(layernorm(h, ln_g, ln_b))


def chunk_attention(q, k, v, rel_bias):
    b, s, h, hd = q.shape
    nc = s // CHUNK
    qc = q.reshape(b, nc, CHUNK, h, hd)
    pad = ((0, 0), (LEFT_CHUNKS, 0), (0, 0), (0, 0), (0, 0))
    kp = jnp.pad(k.reshape(b, nc, CHUNK, h, hd), pad)
    vp = jnp.pad(v.reshape(b, nc, CHUNK, h, hd), pad)
    band_idx = jnp.arange(nc)[:, None] + jnp.arange(BAND)[None, :]
    kb = kp[:, band_idx].reshape(b, nc, BAND * CHUNK, h, hd)
    vb = vp[:, band_idx].reshape(b, nc, BAND * CHUNK, h, hd)
    scores = jnp.einsum('bcqhd,bckhd->bhcqk', qc, kb).astype(jnp.float32) * (hd ** -0.5)
    qi = jnp.arange(CHUNK)[:, None]
    km = jnp.arange(BAND * CHUNK)[None, :]
    dist = LEFT_CHUNKS * CHUNK + qi - km
    rel_idx = jnp.clip(dist, -MAX_REL, MAX_REL) + MAX_REL
    bias = rel_bias.astype(jnp.float32)[:, rel_idx]
    valid = jnp.repeat(band_idx >= LEFT_CHUNKS, CHUNK, axis=1)
    scores = scores + bias[None, :, None, :, :]
    scores = jnp.where(valid[None, None, :, None, :], scores, NEG_INF)
    p = jax.nn.softmax(scores, axis=-1).astype(v.dtype)
    out = jnp.einsum('bhcqk,bckhd->bcqhd', p, vb)
    return out.reshape(b, s, h * hd)


def memory_attention(q, km, vm):
    b, s, h, hd = q.shape
    scores = jnp.einsum('bshd,bmhd->bhsm', q, km).astype(jnp.float32) * (hd ** -0.5)
    p = jax.nn.softmax(scores, axis=-1).astype(vm.dtype)
    out = jnp.einsum('bhsm,bmhd->bshd', p, vm)
    return out.reshape(b, s, h * hd)


def setup_inputs(seed: int = 0) -> dict:
    key = jax.random.key(seed)
    ks = jax.random.split(key, 24)

    def nrm(k, shape, scale):
        return jax.random.normal(k, shape, jnp.float32) * scale

    def gain(k, shape):
        return 1.0 + 0.05 * jax.random.normal(k, shape, jnp.float32)

    L = DEPTH
    return {
        'x': nrm(ks[0], (BATCH, SEQ, D_MODEL), 1.0),
        'mem': nrm(ks[1], (BATCH, N_MEM, D_MODEL), 1.0),
        'mix_norm_g': gain(ks[2], (L, D_MODEL)),
        'mem_norm_g': gain(ks[3], (L, D_MODEL)),
        'w_in': nrm(ks[4], (L, D_MODEL, D_IN), D_MODEL ** -0.5),
        'gate_b': nrm(ks[5], (L, N_BRANCH * D_MODEL), 0.1),
        'conv_w': nrm(ks[6], (L, CONV_WIDTH, D_CONV), CONV_WIDTH ** -0.5),
        'conv_b': nrm(ks[7], (L, D_CONV), 0.02),
        'conv_ln_g': gain(ks[8], (L, D_CONV)),
        'conv_ln_b': nrm(ks[9], (L, D_CONV), 0.02),
        'w_conv_out': nrm(ks[10], (L, D_CONV, D_MODEL), D_CONV ** -0.5),
        'rel_bias': nrm(ks[11], (L, N_ATTN_HEADS, 2 * MAX_REL + 1), 0.1),
        'w_attn_out': nrm(ks[12], (L, D_ATTN, D_MODEL), D_ATTN ** -0.5),
        'w_mem_kv': nrm(ks[13], (L, D_MODEL, 2 * D_MEM), D_MODEL ** -0.5),
        'w_mem_out': nrm(ks[14], (L, D_MEM, D_MODEL), D_MEM ** -0.5),
        'w_o': nrm(ks[15], (L, D_MODEL, D_MODEL), D_MODEL ** -0.5),
        'ffn_norm_g': gain(ks[16], (L, D_MODEL)),
        'w_up': nrm(ks[17], (L, D_MODEL, 2 * D_FF), D_MODEL ** -0.5),
        'ffn_conv_w': nrm(ks[18], (L, FFN_CONV_WIDTH, 2 * D_FF), FFN_CONV_WIDTH ** -0.5),
        'ffn_conv_b': nrm(ks[19], (L, 2 * D_FF), 0.02),
        'w_down': nrm(ks[20], (L, D_FF, D_MODEL), D_FF ** -0.5),
        'final_norm_g': gain(ks[21], (D_MODEL,)),
    }


def reference(x, mem, mix_norm_g, mem_norm_g, w_in, gate_b, conv_w, conv_b, conv_ln_g, conv_ln_b,
              w_conv_out, rel_bias, w_attn_out, w_mem_kv, w_mem_out, w_o, ffn_norm_g, w_up,
              ffn_conv_w, ffn_conv_b, w_down, final_norm_g):
    b, s, d = x.shape
    h = x
    for l in range(DEPTH):
        xn = rmsnorm(h, mix_norm_g[l])
        proj = xn @ w_in[l]
        u_conv, qkv, q_mem, gates = jnp.split(proj, SPLITS, axis=-1)
        y_conv = conv_module(u_conv, conv_w[l], conv_b[l], conv_ln_g[l], conv_ln_b[l]) @ w_conv_out[l]
        q, k, v = jnp.split(qkv.reshape(b, s, 3, N_ATTN_HEADS, ATTN_HEAD_DIM), 3, axis=2)
        y_attn = chunk_attention(q[:, :, 0], k[:, :, 0], v[:, :, 0], rel_bias[l]) @ w_attn_out[l]
        mn = rmsnorm(mem, mem_norm_g[l])
        kv = (mn @ w_mem_kv[l]).reshape(b, mem.shape[1], 2, N_MEM_HEADS, MEM_HEAD_DIM)
        qm = q_mem.reshape(b, s, N_MEM_HEADS, MEM_HEAD_DIM)
        y_mem = memory_attention(qm, kv[:, :, 0], kv[:, :, 1]) @ w_mem_out[l]
        g = jax.nn.sigmoid(gates + gate_b[l]).reshape(b, s, N_BRANCH, d)
        merged = g[:, :, 0] * y_conv + g[:, :, 1] * y_attn + g[:, :, 2] * y_mem
        h = h + merged @ w_o[l]
        hn = rmsnorm(h, ffn_norm_g[l])
        up = causal_dwconv(hn @ w_up[l], ffn_conv_w[l], ffn_conv_b[l])
        val, gt = jnp.split(up, 2, axis=-1)
        h = h + (jax.nn.silu(gt) * val) @ w_down[l]
    return rmsnorm(h, final_norm_g)
```

```python
import functools

import jax
import jax.numpy as jnp
import numpy as np
from jax import lax
from jax.experimental import pallas as pl
from jax.experimental.pallas import tpu as pltpu

D_MODEL = 2048
DEPTH = 4
CHUNK = 64
N_MEM = 256
N_BRANCH = 3
D_CONV = D_MODEL // 2
CONV_WIDTH = 31
ATTN_HEAD_DIM = 64
D_ATTN = D_MODEL // 2
N_ATTN_HEADS = D_ATTN // ATTN_HEAD_DIM
LEFT_CHUNKS = 8
BAND = LEFT_CHUNKS + 1
MAX_REL = 256
N_MEM_HEADS = 4
D_MEM = D_MODEL // 2
MEM_HEAD_DIM = D_MEM // N_MEM_HEADS
D_FF = ((8 * D_MODEL // 3 + 255) // 256) * 256
FFN_CONV_WIDTH = 3
D_IN = 2 * D_CONV + 3 * D_ATTN + D_MEM + N_BRANCH * D_MODEL
EPS = 1e-6
NEG_INF = -1e30

COL_GLU_A = 0
COL_GLU_G = D_CONV
COL_Q = 2 * D_CONV
COL_K = COL_Q + D_ATTN
COL_V = COL_K + D_ATTN
COL_QMEM = COL_V + D_ATTN
COL_GATES = COL_QMEM + D_MEM

V7X_LANES = 128
V7X_SCOPED_VMEM_BYTES = 60000 * 1024
BF16_SUBLANES = 16

F32 = jnp.float32
BF16 = jnp.bfloat16


def _compiler_params(semantics):
    return pltpu.CompilerParams(dimension_semantics=semantics, vmem_limit_bytes=V7X_SCOPED_VMEM_BYTES)


def _rms(x, g):
    ms = jnp.mean(x * x, axis=-1, keepdims=True)
    return (x * lax.rsqrt(ms + EPS)) * g


def _norm_matmul_kernel(x_ref, g_ref, w_ref, o_ref, xn_ref):
    @pl.when(pl.program_id(1) == 0)
    def _():
        xn_ref[...] = _rms(x_ref[...], g_ref[...]).astype(BF16)

    o_ref[...] = jnp.dot(xn_ref[...], w_ref[...], preferred_element_type=F32).astype(o_ref.dtype)


def _norm_matmul(x, g, w, layer, *, tm, tn):
    t, d = x.shape
    n = w.shape[-1]
    return pl.pallas_call(
        _norm_matmul_kernel,
        grid=(t // tm, n // tn),
        in_specs=[
            pl.BlockSpec((tm, d), lambda i, j: (i, 0)),
            pl.BlockSpec((None, 1, d), lambda i, j: (layer, 0, 0)),
            pl.BlockSpec((None, d, tn), lambda i, j: (layer, 0, j)),
        ],
        out_specs=pl.BlockSpec((tm, tn), lambda i, j: (i, j)),
        out_shape=jax.ShapeDtypeStruct((t, n), BF16),
        scratch_shapes=[pltpu.VMEM((tm, d), BF16)],
        compiler_params=_compiler_params(("arbitrary", "arbitrary")),
        name="norm_matmul",
    )(x, g, w)


F32_SUBLANES = 8
CONV_HALO = 32
CONV_ROWS = 64
CONV_BASE = CONV_HALO - (CONV_WIDTH - 1)
CONV_SHIFT_ROWS = CONV_HALO - F32_SUBLANES


def _conv_kernel(a_ref, gt_ref, ah_ref, gh_ref, cw_ref, cb_ref, lg_ref, lb_ref, o_ref, hbuf_ref, hs_ref, y_ref,
                 *, tiles_per_seq):
    tm = a_ref.shape[0]
    first = (pl.program_id(0) % tiles_per_seq) == 0

    def glu(a, g):
        return a.astype(F32) * jax.nn.sigmoid(g.astype(F32))

    halo = glu(ah_ref[...], gh_ref[...])
    hbuf_ref[0:CONV_HALO, :] = jnp.where(first, 0.0, halo)
    hbuf_ref[CONV_HALO:, :] = glu(a_ref[...], gt_ref[...])

    for s in range(1, F32_SUBLANES):
        hs_ref[s - 1] = hbuf_ref[pl.ds(s, tm + CONV_SHIFT_ROWS), :]

    n_lane_blocks = D_CONV // V7X_LANES

    def step(i, carry):
        r0 = pl.multiple_of((i // n_lane_blocks) * CONV_ROWS, CONV_ROWS)
        lanes = pl.ds(pl.multiple_of((i % n_lane_blocks) * V7X_LANES, V7X_LANES), V7X_LANES)
        acc = jnp.broadcast_to(cb_ref[:, lanes], (CONV_ROWS, V7X_LANES))
        for s in range(F32_SUBLANES):
            taps = [k for k in range(CONV_WIDTH) if (CONV_BASE + k) % F32_SUBLANES == s]
            span = CONV_ROWS + max(CONV_BASE + k - s for k in taps)
            if s == 0:
                xs = hbuf_ref[pl.ds(r0, span), lanes]
            else:
                xs = hs_ref[s - 1, pl.ds(r0, span), lanes]
            for k in taps:
                a = CONV_BASE + k - s
                acc = acc + cw_ref[k:k + 1, lanes] * xs[a:a + CONV_ROWS, :]
        y_ref[pl.ds(r0, CONV_ROWS), lanes] = acc
        return carry

    lax.fori_loop(0, (tm // CONV_ROWS) * n_lane_blocks, step, 0)

    y = y_ref[...]
    mu = jnp.mean(y, axis=-1, keepdims=True)
    yc = y - mu
    var = jnp.mean(yc * yc, axis=-1, keepdims=True)
    z = yc * lax.rsqrt(var + EPS) * lg_ref[...] + lb_ref[...]
    o_ref[...] = (z * jax.nn.sigmoid(z)).astype(o_ref.dtype)


def _conv_branch(proj, conv_w, conv_b, ln_g, ln_b, layer, *, tm, seq):
    t = proj.shape[0]
    halo_blocks = tm // CONV_HALO
    a_blk = COL_GLU_A // D_CONV
    g_blk = COL_GLU_G // D_CONV

    def halo_idx(col):
        return lambda i: (jnp.maximum(i * halo_blocks - 1, 0), col)

    return pl.pallas_call(
        functools.partial(_conv_kernel, tiles_per_seq=seq // tm),
        grid=(t // tm,),
        in_specs=[
            pl.BlockSpec((tm, D_CONV), lambda i: (i, a_blk)),
            pl.BlockSpec((tm, D_CONV), lambda i: (i, g_blk)),
            pl.BlockSpec((CONV_HALO, D_CONV), halo_idx(a_blk)),
            pl.BlockSpec((CONV_HALO, D_CONV), halo_idx(g_blk)),
            pl.BlockSpec((None, CONV_WIDTH, D_CONV), lambda i: (layer, 0, 0)),
            pl.BlockSpec((None, 1, D_CONV), lambda i: (layer, 0, 0)),
            pl.BlockSpec((None, 1, D_CONV), lambda i: (layer, 0, 0)),
            pl.BlockSpec((None, 1, D_CONV), lambda i: (layer, 0, 0)),
        ],
        out_specs=pl.BlockSpec((tm, D_CONV), lambda i: (i, 0)),
        out_shape=jax.ShapeDtypeStruct((t, D_CONV), BF16),
        scratch_shapes=[
            pltpu.VMEM((tm + CONV_HALO, D_CONV), F32),
            pltpu.VMEM((F32_SUBLANES - 1, tm + CONV_SHIFT_ROWS, D_CONV), F32),
            pltpu.VMEM((tm, D_CONV), F32),
        ],
        compiler_params=_compiler_params(("arbitrary",)),
        name="conv_branch",
    )(proj, proj, proj, proj, conv_w, conv_b, ln_g, ln_b)


ATTN_BLOCK = LEFT_CHUNKS * CHUNK
ATTN_SUB = 2 * CHUNK
ATTN_WIN = ATTN_SUB + LEFT_CHUNKS * CHUNK
HEADS_PER_STEP = V7X_LANES // ATTN_HEAD_DIM


def _attn_bias_table(rel_bias):
    r = np.arange(ATTN_SUB)[:, None]
    kk = np.arange(ATTN_WIN)[None, :]
    cq, qi = r // CHUNK, r % CHUNK
    km = kk - cq * CHUNK
    in_band = (km >= 0) & (km < BAND * CHUNK)
    dist = LEFT_CHUNKS * CHUNK + qi - km
    rel_idx = np.clip(dist, -MAX_REL, MAX_REL) + MAX_REL
    table = rel_bias.astype(F32)[:, :, rel_idx]
    return jnp.where(in_band[None, None], table, NEG_INF)


def _attn_kernel(q_ref, km_ref, kh_ref, vm_ref, vh_ref, bias_ref, o_ref, kbuf_ref, vbuf_ref, *, blocks_per_seq):
    first = (pl.program_id(2) == 0)
    kbuf_ref[0:ATTN_BLOCK, :] = kh_ref[...]
    kbuf_ref[ATTN_BLOCK:, :] = km_ref[...]
    vbuf_ref[0:ATTN_BLOCK, :] = vh_ref[...]
    vbuf_ref[ATTN_BLOCK:, :] = vm_ref[...]

    lane = lax.broadcasted_iota(jnp.int32, (1, V7X_LANES), 1)
    col = lax.broadcasted_iota(jnp.int32, (1, ATTN_WIN), 1)
    scale = ATTN_HEAD_DIM ** -0.5

    def compute(mask_left):
        for j in range(ATTN_BLOCK // ATTN_SUB):
            rows = slice(j * ATTN_SUB, (j + 1) * ATTN_SUB)
            win = slice(j * ATTN_SUB, j * ATTN_SUB + ATTN_WIN)
            q = q_ref[rows, :] * scale
            kwin = kbuf_ref[win, :]
            vwin = vbuf_ref[win, :]
            outs = []
            for h in range(HEADS_PER_STEP):
                head_lanes = (lane >= h * ATTN_HEAD_DIM) & (lane < (h + 1) * ATTN_HEAD_DIM)
                qh = jnp.where(head_lanes, q, jnp.zeros_like(q))
                s = lax.dot_general(qh, kwin, (((1,), (1,)), ((), ())), preferred_element_type=F32)
                s = s + bias_ref[h]
                if mask_left:
                    s = jnp.where(col < ATTN_BLOCK - j * ATTN_SUB, NEG_INF, s)
                m = jnp.max(s, axis=-1, keepdims=True)
                p = jnp.exp(s - m)
                l = jnp.sum(p, axis=-1, keepdims=True)
                o = jnp.dot(p.astype(BF16), vwin, preferred_element_type=F32)
                outs.append(o / l)
            out = outs[0]
            for h in range(1, HEADS_PER_STEP):
                out = jnp.where(lane >= h * ATTN_HEAD_DIM, outs[h], out)
            o_ref[rows, :] = out.astype(o_ref.dtype)

    @pl.when(first)
    def _():
        compute(True)

    @pl.when(jnp.logical_not(first))
    def _():
        compute(False)


def _chunk_attention(proj, bias_table, layer, *, seq, batch):
    t = proj.shape[0]
    bps = seq // ATTN_BLOCK
    q_blk, k_blk, v_blk = COL_Q // V7X_LANES, COL_K // V7X_LANES, COL_V // V7X_LANES

    def main_idx(col0):
        return lambda hp, b, i: (b * bps + i, col0 + hp)

    def halo_idx(col0):
        return lambda hp, b, i: (b * bps + jnp.maximum(i - 1, 0), col0 + hp)

    blk = (ATTN_BLOCK, V7X_LANES)
    return pl.pallas_call(
        functools.partial(_attn_kernel, blocks_per_seq=bps),
        grid=(N_ATTN_HEADS // HEADS_PER_STEP, batch, bps),
        in_specs=[
            pl.BlockSpec(blk, main_idx(q_blk)),
            pl.BlockSpec(blk, main_idx(k_blk)),
            pl.BlockSpec(blk, halo_idx(k_blk)),
            pl.BlockSpec(blk, main_idx(v_blk)),
            pl.BlockSpec(blk, halo_idx(v_blk)),
            pl.BlockSpec((None, HEADS_PER_STEP, ATTN_SUB, ATTN_WIN), lambda hp, b, i: (layer, hp, 0, 0)),
        ],
        out_specs=pl.BlockSpec(blk, lambda hp, b, i: (b * bps + i, hp)),
        out_shape=jax.ShapeDtypeStruct((t, D_ATTN), BF16),
        scratch_shapes=[pltpu.VMEM((2 * ATTN_BLOCK, V7X_LANES), BF16), pltpu.VMEM((2 * ATTN_BLOCK, V7X_LANES), BF16)],
        compiler_params=_compiler_params(("arbitrary", "arbitrary", "arbitrary")),
        name="chunk_attention",
    )(proj, proj, proj, proj, proj, bias_table)


def _mem_attn_kernel(q_ref, kv_ref, o_ref):
    scale = MEM_HEAD_DIM ** -0.5
    for h in range(N_MEM_HEADS):
        lanes = slice(h * MEM_HEAD_DIM, (h + 1) * MEM_HEAD_DIM)
        q = q_ref[:, lanes] * scale
        k = kv_ref[:, lanes]
        v = kv_ref[:, D_MEM + h * MEM_HEAD_DIM:D_MEM + (h + 1) * MEM_HEAD_DIM]
        s = lax.dot_general(q, k, (((1,), (1,)), ((), ())), preferred_element_type=F32)
        m = jnp.max(s, axis=-1, keepdims=True)
        p = jnp.exp(s - m)
        l = jnp.sum(p, axis=-1, keepdims=True)
        o = jnp.dot(p.astype(BF16), v, preferred_element_type=F32)
        o_ref[:, lanes] = (o / l).astype(o_ref.dtype)


def _mem_attention(proj, kv, *, tm, seq, batch):
    t = proj.shape[0]
    tiles = seq // tm
    return pl.pallas_call(
        _mem_attn_kernel,
        grid=(batch, tiles),
        in_specs=[
            pl.BlockSpec((tm, D_MEM), lambda b, i: (b * tiles + i, COL_QMEM // D_MEM)),
            pl.BlockSpec((N_MEM, 2 * D_MEM), lambda b, i: (b, 0)),
        ],
        out_specs=pl.BlockSpec((tm, D_MEM), lambda b, i: (b * tiles + i, 0)),
        out_shape=jax.ShapeDtypeStruct((t, D_MEM), BF16),
        compiler_params=_compiler_params(("arbitrary", "arbitrary")),
        name="mem_attention",
    )(proj, kv)


def _merge_kernel(h_ref, c_ref, a_ref, m_ref, g0_ref, g1_ref, g2_ref, b0_ref, b1_ref, b2_ref,
                  wc_ref, wa_ref, wm_ref, wo_ref, o_ref):
    j = pl.program_id(1)

    def gated(x_ref, w_ref, g_ref, b_ref):
        y = jnp.dot(x_ref[...], w_ref[...], preferred_element_type=F32)
        return jax.nn.sigmoid(g_ref[...].astype(F32) + b_ref[...]) * y

    merged = (gated(c_ref, wc_ref, g0_ref, b0_ref) + gated(a_ref, wa_ref, g1_ref, b1_ref)
              + gated(m_ref, wm_ref, g2_ref, b2_ref))
    contrib = jnp.dot(merged.astype(BF16), wo_ref[...], preferred_element_type=F32)

    @pl.when(j == 0)
    def _():
        o_ref[...] = h_ref[...] + contrib

    @pl.when(j != 0)
    def _():
        o_ref[...] += contrib


def _merge(h, c, a, m, proj, gate_b, w_conv_out, w_attn_out, w_mem_out, w_o, layer, *, tm, tn):
    t = h.shape[0]
    nj = D_MODEL // tn
    gate_blk = COL_GATES // tn

    def gate_idx(branch):
        return lambda i, j: (i, gate_blk + branch * nj + j)

    def bias_idx(branch):
        return lambda i, j: (layer, 0, branch * nj + j)

    row_spec = pl.BlockSpec((tm, D_CONV), lambda i, j: (i, 0))
    w_spec = pl.BlockSpec((None, D_CONV, tn), lambda i, j: (layer, 0, j))
    return pl.pallas_call(
        _merge_kernel,
        grid=(t // tm, nj),
        in_specs=[
            pl.BlockSpec((tm, D_MODEL), lambda i, j: (i, 0)),
            row_spec, row_spec, row_spec,
            pl.BlockSpec((tm, tn), gate_idx(0)),
            pl.BlockSpec((tm, tn), gate_idx(1)),
            pl.BlockSpec((tm, tn), gate_idx(2)),
            pl.BlockSpec((None, 1, tn), bias_idx(0)),
            pl.BlockSpec((None, 1, tn), bias_idx(1)),
            pl.BlockSpec((None, 1, tn), bias_idx(2)),
            w_spec, w_spec, w_spec,
            pl.BlockSpec((None, tn, D_MODEL), lambda i, j: (layer, j, 0)),
        ],
        out_specs=pl.BlockSpec((tm, D_MODEL), lambda i, j: (i, 0)),
        out_shape=jax.ShapeDtypeStruct((t, D_MODEL), F32),
        compiler_params=_compiler_params(("arbitrary", "arbitrary")),
        name="merge",
    )(h, c, a, m, proj, proj, proj, gate_b, gate_b, gate_b, w_conv_out, w_attn_out, w_mem_out, w_o)


FFN_HALO = BF16_SUBLANES


def _ffn_kernel(h_ref, halo_ref, g_ref, wv_ref, wg_ref, cwv_ref, cwg_ref, cbv_ref, cbg_ref, wd_ref, fg_ref,
                o_ref, hn_ref, uv_ref, ug_ref, *, tiles_per_seq, final_norm):
    tm = h_ref.shape[0]
    i, j = pl.program_id(0), pl.program_id(1)

    @pl.when(j == 0)
    def _():
        first = (i % tiles_per_seq) == 0
        halo = _rms(halo_ref[...], g_ref[...])
        hn_ref[0:FFN_HALO, :] = jnp.where(first, 0.0, halo).astype(BF16)
        hn_ref[FFN_HALO:, :] = _rms(h_ref[...], g_ref[...]).astype(BF16)

    hn = hn_ref[...]
    uv_ref[...] = jnp.dot(hn, wv_ref[...], preferred_element_type=F32)
    ug_ref[...] = jnp.dot(hn, wg_ref[...], preferred_element_type=F32)

    def dwconv(u_ref, cw_ref, cb_ref):
        acc = cb_ref[...]
        for k in range(FFN_CONV_WIDTH):
            acc = acc + cw_ref[k:k + 1, :] * u_ref[pl.ds(FFN_HALO - (FFN_CONV_WIDTH - 1) + k, tm), :]
        return acc

    val = dwconv(uv_ref, cwv_ref, cbv_ref)
    gt = dwconv(ug_ref, cwg_ref, cbg_ref)
    act = (gt * jax.nn.sigmoid(gt) * val).astype(BF16)
    contrib = jnp.dot(act, wd_ref[...], preferred_element_type=F32)

    @pl.when(j == 0)
    def _():
        o_ref[...] = h_ref[...] + contrib

    @pl.when(j != 0)
    def _():
        o_ref[...] += contrib

    if final_norm:
        @pl.when(j == pl.num_programs(1) - 1)
        def _():
            o_ref[...] = _rms(o_ref[...], fg_ref[...])


def _ffn(h, norm_g, w_up, conv_w, conv_b, w_down, final_g, layer, *, tm, tn, seq, final_norm):
    t = h.shape[0]
    nj = D_FF // tn
    halo_blocks = tm // FFN_HALO
    return pl.pallas_call(
        functools.partial(_ffn_kernel, tiles_per_seq=seq // tm, final_norm=final_norm),
        grid=(t // tm, nj),
        in_specs=[
            pl.BlockSpec((tm, D_MODEL), lambda i, j: (i, 0)),
            pl.BlockSpec((FFN_HALO, D_MODEL), lambda i, j: (jnp.maximum(i * halo_blocks - 1, 0), 0)),
            pl.BlockSpec((None, 1, D_MODEL), lambda i, j: (layer, 0, 0)),
            pl.BlockSpec((None, D_MODEL, tn), lambda i, j: (layer, 0, j)),
            pl.BlockSpec((None, D_MODEL, tn), lambda i, j: (layer, 0, nj + j)),
            pl.BlockSpec((None, FFN_CONV_WIDTH, tn), lambda i, j: (layer, 0, j)),
            pl.BlockSpec((None, FFN_CONV_WIDTH, tn), lambda i, j: (layer, 0, nj + j)),
            pl.BlockSpec((None, 1, tn), lambda i, j: (layer, 0, j)),
            pl.BlockSpec((None, 1, tn), lambda i, j: (layer, 0, nj + j)),
            pl.BlockSpec((None, tn, D_MODEL), lambda i, j: (layer, j, 0)),
            pl.BlockSpec((1, D_MODEL), lambda i, j: (0, 0)),
        ],
        out_specs=pl.BlockSpec((tm, D_MODEL), lambda i, j: (i, 0)),
        out_shape=jax.ShapeDtypeStruct((t, D_MODEL), F32),
        scratch_shapes=[
            pltpu.VMEM((tm + FFN_HALO, D_MODEL), BF16),
            pltpu.VMEM((tm + FFN_HALO, tn), F32),
            pltpu.VMEM((tm + FFN_HALO, tn), F32),
        ],
        compiler_params=_compiler_params(("arbitrary", "arbitrary")),
        name="conv_ffn",
    )(h, h, norm_g, w_up, w_up, conv_w, conv_w, conv_b, conv_b, w_down, final_g)


def kernel(x, mem, mix_norm_g, mem_norm_g, w_in, gate_b, conv_w, conv_b, conv_ln_g, conv_ln_b, w_conv_out,
           rel_bias, w_attn_out, w_mem_kv, w_mem_out, w_o, ffn_norm_g, w_up, ffn_conv_w, ffn_conv_b, w_down,
           final_norm_g):
    batch, seq, d = x.shape
    n_mem = mem.shape[1]
    assert (d, n_mem) == (D_MODEL, N_MEM) and seq % ATTN_BLOCK == 0

    def row(p):
        return p.reshape(p.shape[0], 1, p.shape[1])

    w_in_b, w_conv_out_b, w_attn_out_b = w_in.astype(BF16), w_conv_out.astype(BF16), w_attn_out.astype(BF16)
    w_mem_kv_b, w_mem_out_b, w_o_b = w_mem_kv.astype(BF16), w_mem_out.astype(BF16), w_o.astype(BF16)
    w_up_b, w_down_b = w_up.astype(BF16), w_down.astype(BF16)
    mix_g, mem_g, ffn_g = row(mix_norm_g), row(mem_norm_g), row(ffn_norm_g)
    gate_b3, conv_b3, ln_g3, ln_b3, ffn_cb3 = row(gate_b), row(conv_b), row(conv_ln_g), row(conv_ln_b), row(ffn_conv_b)
    final_g = final_norm_g.reshape(1, D_MODEL)
    bias_table = _attn_bias_table(rel_bias)

    h = x.reshape(batch * seq, d)
    mem2 = mem.reshape(batch * n_mem, d)
    for layer in range(DEPTH):
        proj = _norm_matmul(h, mix_g, w_in_b, layer, tm=1024, tn=1024)
        kv = _norm_matmul(mem2, mem_g, w_mem_kv_b, layer, tm=batch * n_mem, tn=1024)
        c = _conv_branch(proj, conv_w, conv_b3, ln_g3, ln_b3, layer, tm=512, seq=seq)
        a = _chunk_attention(proj, bias_table, layer, seq=seq, batch=batch)
        m = _mem_attention(proj, kv, tm=512, seq=seq, batch=batch)
        h = _merge(h, c, a, m, proj, gate_b3, w_conv_out_b, w_attn_out_b, w_mem_out_b, w_o_b, layer, tm=512, tn=512)
        h = _ffn(h, ffn_g, w_up_b, ffn_conv_w, ffn_cb3, w_down_b, final_g, layer, tm=512, tn=512, seq=seq,
                 final_norm=(layer == DEPTH - 1))
    return h.reshape(batch, seq, d)
```

```python
import functools

import jax
import jax.numpy as jnp
import numpy as np
from jax import lax
from jax.experimental import pallas as pl
from jax.experimental.pallas import tpu as pltpu

D_MODEL = 2048
DEPTH = 4
CHUNK = 64
N_MEM = 256
N_BRANCH = 3
D_CONV = D_MODEL // 2
CONV_WIDTH = 31
ATTN_HEAD_DIM = 64
D_ATTN = D_MODEL // 2
N_ATTN_HEADS = D_ATTN // ATTN_HEAD_DIM
LEFT_CHUNKS = 8
BAND = LEFT_CHUNKS + 1
MAX_REL = 256
N_MEM_HEADS = 4
D_MEM = D_MODEL // 2
MEM_HEAD_DIM = D_MEM // N_MEM_HEADS
D_FF = ((8 * D_MODEL // 3 + 255) // 256) * 256
FFN_CONV_WIDTH = 3
D_IN = 2 * D_CONV + 3 * D_ATTN + D_MEM + N_BRANCH * D_MODEL
EPS = 1e-6
NEG_INF = -1e30

COL_GLU_A = 0
COL_GLU_G = D_CONV
COL_Q = 2 * D_CONV
COL_K = COL_Q + D_ATTN
COL_V = COL_K + D_ATTN
COL_QMEM = COL_V + D_ATTN
COL_GATES = COL_QMEM + D_MEM

V7X_LANES = 128
V7X_SCOPED_VMEM_BYTES = 60000 * 1024
BF16_SUBLANES = 16

F32 = jnp.float32
BF16 = jnp.bfloat16


def _compiler_params(semantics):
    return pltpu.CompilerParams(dimension_semantics=semantics, vmem_limit_bytes=V7X_SCOPED_VMEM_BYTES)


def _rms(x, g):
    ms = jnp.mean(x * x, axis=-1, keepdims=True)
    return (x * lax.rsqrt(ms + EPS)) * g


def _norm_matmul_kernel(x_ref, g_ref, w_ref, o_ref, xn_ref):
    @pl.when(pl.program_id(1) == 0)
    def _():
        xn_ref[...] = _rms(x_ref[...], g_ref[...]).astype(BF16)

    o_ref[...] = jnp.dot(xn_ref[...], w_ref[...], preferred_element_type=F32).astype(o_ref.dtype)


def _norm_matmul(x, g, w, layer, *, tm, tn):
    t, d = x.shape
    n = w.shape[-1]
    return pl.pallas_call(
        _norm_matmul_kernel,
        grid=(t // tm, n // tn),
        in_specs=[
            pl.BlockSpec((tm, d), lambda i, j: (i, 0)),
            pl.BlockSpec((None, 1, d), lambda i, j: (layer, 0, 0)),
            pl.BlockSpec((None, d, tn), lambda i, j: (layer, 0, j)),
        ],
        out_specs=pl.BlockSpec((tm, tn), lambda i, j: (i, j)),
        out_shape=jax.ShapeDtypeStruct((t, n), BF16),
        scratch_shapes=[pltpu.VMEM((tm, d), BF16)],
        compiler_params=_compiler_params(("arbitrary", "arbitrary")),
        name="norm_matmul",
    )(x, g, w)


F32_SUBLANES = 8
CONV_HALO = 32
CONV_ROWS = 64
CONV_BASE = CONV_HALO - (CONV_WIDTH - 1)
CONV_SHIFT_ROWS = CONV_HALO - F32_SUBLANES


def _conv_kernel(a_ref, gt_ref, ah_ref, gh_ref, cw_ref, cb_ref, lg_ref, lb_ref, o_ref, hbuf_ref, hs_ref, y_ref,
                 *, tiles_per_seq):
    tm = a_ref.shape[0]
    first = (pl.program_id(0) % tiles_per_seq) == 0

    def glu(a, g):
        return a.astype(F32) * jax.nn.sigmoid(g.astype(F32))

    halo = glu(ah_ref[...], gh_ref[...])
    hbuf_ref[0:CONV_HALO, :] = jnp.where(first, 0.0, halo)
    hbuf_ref[CONV_HALO:, :] = glu(a_ref[...], gt_ref[...])

    for s in range(1, F32_SUBLANES):
        hs_ref[s - 1] = hbuf_ref[pl.ds(s, tm + CONV_SHIFT_ROWS), :]

    n_lane_blocks = D_CONV // V7X_LANES

    def step(i, carry):
        r0 = pl.multiple_of((i // n_lane_blocks) * CONV_ROWS, CONV_ROWS)
        lanes = pl.ds(pl.multiple_of((i % n_lane_blocks) * V7X_LANES, V7X_LANES), V7X_LANES)
        acc = jnp.broadcast_to(cb_ref[:, lanes], (CONV_ROWS, V7X_LANES))
        for s in range(F32_SUBLANES):
            taps = [k for k in range(CONV_WIDTH) if (CONV_BASE + k) % F32_SUBLANES == s]
            span = CONV_ROWS + max(CONV_BASE + k - s for k in taps)
            if s == 0:
                xs = hbuf_ref[pl.ds(r0, span), lanes]
            else:
                xs = hs_ref[s - 1, pl.ds(r0, span), lanes]
            for k in taps:
                a = CONV_BASE + k - s
                acc = acc + cw_ref[k:k + 1, lanes] * xs[a:a + CONV_ROWS, :]
        y_ref[pl.ds(r0, CONV_ROWS), lanes] = acc
        return carry

    lax.fori_loop(0, (tm // CONV_ROWS) * n_lane_blocks, step, 0)

    y = y_ref[...]
    mu = jnp.mean(y, axis=-1, keepdims=True)
    yc = y - mu
    var = jnp.mean(yc * yc, axis=-1, keepdims=True)
    z = yc * lax.rsqrt(var + EPS) * lg_ref[...] + lb_ref[...]
    o_ref[...] = (z * jax.nn.sigmoid(z)).astype(o_ref.dtype)


def _conv_branch(proj, conv_w, conv_b, ln_g, ln_b, layer, *, tm, seq):
    t = proj.shape[0]
    halo_blocks = tm // CONV_HALO
    a_blk = COL_GLU_A // D_CONV
    g_blk = COL_GLU_G // D_CONV

    def halo_idx(col):
        return lambda i: (jnp.maximum(i * halo_blocks - 1, 0), col)

    return pl.pallas_call(
        functools.partial(_conv_kernel, tiles_per_seq=seq // tm),
        grid=(t // tm,),
        in_specs=[
            pl.BlockSpec((tm, D_CONV), lambda i: (i, a_blk)),
            pl.BlockSpec((tm, D_CONV), lambda i: (i, g_blk)),
            pl.BlockSpec((CONV_HALO, D_CONV), halo_idx(a_blk)),
            pl.BlockSpec((CONV_HALO, D_CONV), halo_idx(g_blk)),
            pl.BlockSpec((None, CONV_WIDTH, D_CONV), lambda i: (layer, 0, 0)),
            pl.BlockSpec((None, 1, D_CONV), lambda i: (layer, 0, 0)),
            pl.BlockSpec((None, 1, D_CONV), lambda i: (layer, 0, 0)),
            pl.BlockSpec((None, 1, D_CONV), lambda i: (layer, 0, 0)),
        ],
        out_specs=pl.BlockSpec((tm, D_CONV), lambda i: (i, 0)),
        out_shape=jax.ShapeDtypeStruct((t, D_CONV), BF16),
        scratch_shapes=[
            pltpu.VMEM((tm + CONV_HALO, D_CONV), F32),
            pltpu.VMEM((F32_SUBLANES - 1, tm + CONV_SHIFT_ROWS, D_CONV), F32),
            pltpu.VMEM((tm, D_CONV), F32),
        ],
        compiler_params=_compiler_params(("arbitrary",)),
        name="conv_branch",
    )(proj, proj, proj, proj, conv_w, conv_b, ln_g, ln_b)


ATTN_BLOCK = LEFT_CHUNKS * CHUNK
ATTN_SUB = 4 * CHUNK
ATTN_WIN = ATTN_SUB + LEFT_CHUNKS * CHUNK
HEADS_PER_STEP = V7X_LANES // ATTN_HEAD_DIM


def _attn_bias_table(rel_bias):
    period = ATTN_SUB + ATTN_WIN - 1
    m = np.arange(period)
    dist = (ATTN_WIN - 1 - m) % period - (ATTN_WIN - 1) + LEFT_CHUNKS * CHUNK
    gp = rel_bias.astype(F32)[:, :, np.clip(dist, -MAX_REL, MAX_REL) + MAX_REL]
    flat = jnp.tile(gp, (1, 1, ATTN_SUB))[:, :, :ATTN_SUB * (period - 1)]
    toeplitz = flat.reshape(flat.shape[0], flat.shape[1], ATTN_SUB, period - 1)[:, :, :, :ATTN_WIN]
    r = np.arange(ATTN_SUB)[:, None]
    km = np.arange(ATTN_WIN)[None, :] - (r // CHUNK) * CHUNK
    in_band = (km >= 0) & (km < BAND * CHUNK)
    table = jnp.where(in_band[None, None], toeplitz, NEG_INF)
    return table.reshape(table.shape[0], N_ATTN_HEADS // HEADS_PER_STEP, HEADS_PER_STEP * ATTN_SUB, ATTN_WIN)


def _attn_kernel(q_ref, km_ref, kh_ref, vm_ref, vh_ref, bias_ref, o_ref, kbuf_ref, vbuf_ref):
    first = (pl.program_id(2) == 0)
    kbuf_ref[0:ATTN_BLOCK, :] = kh_ref[...]
    kbuf_ref[ATTN_BLOCK:, :] = km_ref[...]
    vbuf_ref[0:ATTN_BLOCK, :] = vh_ref[...]
    vbuf_ref[ATTN_BLOCK:, :] = vm_ref[...]

    lane = lax.broadcasted_iota(jnp.int32, (1, V7X_LANES), 1)
    col = lax.broadcasted_iota(jnp.int32, (1, ATTN_WIN), 1)
    scale = ATTN_HEAD_DIM ** -0.5

    def compute(mask_left):
        for j in range(ATTN_BLOCK // ATTN_SUB):
            rows = slice(j * ATTN_SUB, (j + 1) * ATTN_SUB)
            win = slice(j * ATTN_SUB, j * ATTN_SUB + ATTN_WIN)
            q = q_ref[rows, :] * scale
            qs = jnp.concatenate(
                [jnp.where((lane >= h * ATTN_HEAD_DIM) & (lane < (h + 1) * ATTN_HEAD_DIM), q, jnp.zeros_like(q))
                 for h in range(HEADS_PER_STEP)], axis=0)
            s = lax.dot_general(qs, kbuf_ref[win, :], (((1,), (1,)), ((), ())), preferred_element_type=F32)
            s = s + bias_ref[...]
            if mask_left:
                s = jnp.where(col < ATTN_BLOCK - j * ATTN_SUB, NEG_INF, s)
            m = jnp.max(s, axis=-1, keepdims=True)
            p = jnp.exp(s - m)
            l = jnp.sum(p, axis=-1, keepdims=True)
            o = jnp.dot(p.astype(BF16), vbuf_ref[win, :], preferred_element_type=F32) / l
            out = o[0:ATTN_SUB]
            for h in range(1, HEADS_PER_STEP):
                out = jnp.where(lane >= h * ATTN_HEAD_DIM, o[h * ATTN_SUB:(h + 1) * ATTN_SUB], out)
            o_ref[rows, :] = out.astype(o_ref.dtype)

    @pl.when(first)
    def _():
        compute(True)

    @pl.when(jnp.logical_not(first))
    def _():
        compute(False)


def _chunk_attention(proj, bias_table, layer, *, seq, batch):
    t = proj.shape[0]
    bps = seq // ATTN_BLOCK
    q_blk, k_blk, v_blk = COL_Q // V7X_LANES, COL_K // V7X_LANES, COL_V // V7X_LANES

    def main_idx(col0):
        return lambda hp, b, i: (b * bps + i, col0 + hp)

    def halo_idx(col0):
        return lambda hp, b, i: (b * bps + jnp.maximum(i - 1, 0), col0 + hp)

    blk = (ATTN_BLOCK, V7X_LANES)
    return pl.pallas_call(
        _attn_kernel,
        grid=(N_ATTN_HEADS // HEADS_PER_STEP, batch, bps),
        in_specs=[
            pl.BlockSpec(blk, main_idx(q_blk)),
            pl.BlockSpec(blk, main_idx(k_blk)),
            pl.BlockSpec(blk, halo_idx(k_blk)),
            pl.BlockSpec(blk, main_idx(v_blk)),
            pl.BlockSpec(blk, halo_idx(v_blk)),
            pl.BlockSpec((None, None, HEADS_PER_STEP * ATTN_SUB, ATTN_WIN), lambda hp, b, i: (layer, hp, 0, 0)),
        ],
        out_specs=pl.BlockSpec(blk, lambda hp, b, i: (b * bps + i, hp)),
        out_shape=jax.ShapeDtypeStruct((t, D_ATTN), BF16),
        scratch_shapes=[pltpu.VMEM((2 * ATTN_BLOCK, V7X_LANES), BF16), pltpu.VMEM((2 * ATTN_BLOCK, V7X_LANES), BF16)],
        compiler_params=_compiler_params(("arbitrary", "arbitrary", "arbitrary")),
        name="chunk_attention",
    )(proj, proj, proj, proj, proj, bias_table)


def _mem_attn_kernel(q_ref, kv_ref, o_ref):
    scale = MEM_HEAD_DIM ** -0.5
    for h in range(N_MEM_HEADS):
        lanes = slice(h * MEM_HEAD_DIM, (h + 1) * MEM_HEAD_DIM)
        q = q_ref[:, lanes] * scale
        k = kv_ref[:, lanes]
        v = kv_ref[:, D_MEM + h * MEM_HEAD_DIM:D_MEM + (h + 1) * MEM_HEAD_DIM]
        s = lax.dot_general(q, k, (((1,), (1,)), ((), ())), preferred_element_type=F32)
        m = jnp.max(s, axis=-1, keepdims=True)
        p = jnp.exp(s - m)
        l = jnp.sum(p, axis=-1, keepdims=True)
        o = jnp.dot(p.astype(BF16), v, preferred_element_type=F32)
        o_ref[:, lanes] = (o / l).astype(o_ref.dtype)


def _mem_attention(proj, kv, *, tm, seq, batch):
    t = proj.shape[0]
    tiles = seq // tm
    return pl.pallas_call(
        _mem_attn_kernel,
        grid=(batch, tiles),
        in_specs=[
            pl.BlockSpec((tm, D_MEM), lambda b, i: (b * tiles + i, COL_QMEM // D_MEM)),
            pl.BlockSpec((N_MEM, 2 * D_MEM), lambda b, i: (b, 0)),
        ],
        out_specs=pl.BlockSpec((tm, D_MEM), lambda b, i: (b * tiles + i, 0)),
        out_shape=jax.ShapeDtypeStruct((t, D_MEM), BF16),
        compiler_params=_compiler_params(("arbitrary", "arbitrary")),
        name="mem_attention",
    )(proj, kv)


def _merge_kernel(h_ref, c_ref, a_ref, m_ref, g0_ref, g1_ref, g2_ref, b0_ref, b1_ref, b2_ref,
                  wc_ref, wa_ref, wm_ref, wo_ref, o_ref):
    j = pl.program_id(1)

    def gated(x_ref, w_ref, g_ref, b_ref):
        y = jnp.dot(x_ref[...], w_ref[...], preferred_element_type=F32)
        return jax.nn.sigmoid(g_ref[...].astype(F32) + b_ref[...]) * y

    merged = (gated(c_ref, wc_ref, g0_ref, b0_ref) + gated(a_ref, wa_ref, g1_ref, b1_ref)
              + gated(m_ref, wm_ref, g2_ref, b2_ref))
    contrib = jnp.dot(merged.astype(BF16), wo_ref[...], preferred_element_type=F32)

    @pl.when(j == 0)
    def _():
        o_ref[...] = h_ref[...] + contrib

    @pl.when(j != 0)
    def _():
        o_ref[...] += contrib


def _merge(h, c, a, m, proj, gate_b, w_conv_out, w_attn_out, w_mem_out, w_o, layer, *, tm, tn):
    t = h.shape[0]
    nj = D_MODEL // tn
    gate_blk = COL_GATES // tn

    def gate_idx(branch):
        return lambda i, j: (i, gate_blk + branch * nj + j)

    def bias_idx(branch):
        return lambda i, j: (layer, 0, branch * nj + j)

    row_spec = pl.BlockSpec((tm, D_CONV), lambda i, j: (i, 0))
    w_spec = pl.BlockSpec((None, D_CONV, tn), lambda i, j: (layer, 0, j))
    return pl.pallas_call(
        _merge_kernel,
        grid=(t // tm, nj),
        in_specs=[
            pl.BlockSpec((tm, D_MODEL), lambda i, j: (i, 0), pipeline_mode=pl.Buffered(1)),
            row_spec, row_spec, row_spec,
            pl.BlockSpec((tm, tn), gate_idx(0)),
            pl.BlockSpec((tm, tn), gate_idx(1)),
            pl.BlockSpec((tm, tn), gate_idx(2)),
            pl.BlockSpec((None, 1, tn), bias_idx(0)),
            pl.BlockSpec((None, 1, tn), bias_idx(1)),
            pl.BlockSpec((None, 1, tn), bias_idx(2)),
            w_spec, w_spec, w_spec,
            pl.BlockSpec((None, tn, D_MODEL), lambda i, j: (layer, j, 0)),
        ],
        out_specs=pl.BlockSpec((tm, D_MODEL), lambda i, j: (i, 0), pipeline_mode=pl.Buffered(1)),
        out_shape=jax.ShapeDtypeStruct((t, D_MODEL), F32),
        compiler_params=_compiler_params(("arbitrary", "arbitrary")),
        name="merge",
    )(h, c, a, m, proj, proj, proj, gate_b, gate_b, gate_b, w_conv_out, w_attn_out, w_mem_out, w_o)


FFN_HALO = BF16_SUBLANES


def _ffn_kernel(h_ref, halo_ref, g_ref, wv_ref, wg_ref, cwv_ref, cwg_ref, cbv_ref, cbg_ref, wd_ref, fg_ref,
                o_ref, hn_ref, uv_ref, ug_ref, *, tiles_per_seq, final_norm):
    tm = h_ref.shape[0]
    i, j = pl.program_id(0), pl.program_id(1)

    @pl.when(j == 0)
    def _():
        first = (i % tiles_per_seq) == 0
        halo = _rms(halo_ref[...], g_ref[...])
        hn_ref[0:FFN_HALO, :] = jnp.where(first, 0.0, halo).astype(BF16)
        hn_ref[FFN_HALO:, :] = _rms(h_ref[...], g_ref[...]).astype(BF16)

    hn = hn_ref[...]
    uv_ref[...] = jnp.dot(hn, wv_ref[...], preferred_element_type=F32)
    ug_ref[...] = jnp.dot(hn, wg_ref[...], preferred_element_type=F32)

    def dwconv(u_ref, cw_ref, cb_ref):
        acc = cb_ref[...]
        for k in range(FFN_CONV_WIDTH):
            acc = acc + cw_ref[k:k + 1, :] * u_ref[pl.ds(FFN_HALO - (FFN_CONV_WIDTH - 1) + k, tm), :]
        return acc

    val = dwconv(uv_ref, cwv_ref, cbv_ref)
    gt = dwconv(ug_ref, cwg_ref, cbg_ref)
    act = (gt * jax.nn.sigmoid(gt) * val).astype(BF16)
    contrib = jnp.dot(act, wd_ref[...], preferred_element_type=F32)

    @pl.when(j == 0)
    def _():
        o_ref[...] = h_ref[...] + contrib

    @pl.when(j != 0)
    def _():
        o_ref[...] += contrib

    if final_norm:
        @pl.when(j == pl.num_programs(1) - 1)
        def _():
            o_ref[...] = _rms(o_ref[...], fg_ref[...])


def _ffn(h, norm_g, w_up, conv_w, conv_b, w_down, final_g, layer, *, tm, tn, seq, final_norm):
    t = h.shape[0]
    nj = D_FF // tn
    halo_blocks = tm // FFN_HALO
    return pl.pallas_call(
        functools.partial(_ffn_kernel, tiles_per_seq=seq // tm, final_norm=final_norm),
        grid=(t // tm, nj),
        in_specs=[
            pl.BlockSpec((tm, D_MODEL), lambda i, j: (i, 0), pipeline_mode=pl.Buffered(1)),
            pl.BlockSpec((FFN_HALO, D_MODEL), lambda i, j: (jnp.maximum(i * halo_blocks - 1, 0), 0)),
            pl.BlockSpec((None, 1, D_MODEL), lambda i, j: (layer, 0, 0)),
            pl.BlockSpec((None, D_MODEL, tn), lambda i, j: (layer, 0, j)),
            pl.BlockSpec((None, D_MODEL, tn), lambda i, j: (layer, 0, nj + j)),
            pl.BlockSpec((None, FFN_CONV_WIDTH, tn), lambda i, j: (layer, 0, j)),
            pl.BlockSpec((None, FFN_CONV_WIDTH, tn), lambda i, j: (layer, 0, nj + j)),
            pl.BlockSpec((None, 1, tn), lambda i, j: (layer, 0, j)),
            pl.BlockSpec((None, 1, tn), lambda i, j: (layer, 0, nj + j)),
            pl.BlockSpec((None, tn, D_MODEL), lambda i, j: (layer, j, 0)),
            pl.BlockSpec((1, D_MODEL), lambda i, j: (0, 0)),
        ],
        out_specs=pl.BlockSpec((tm, D_MODEL), lambda i, j: (i, 0), pipeline_mode=pl.Buffered(1)),
        out_shape=jax.ShapeDtypeStruct((t, D_MODEL), F32),
        scratch_shapes=[
            pltpu.VMEM((tm + FFN_HALO, D_MODEL), BF16),
            pltpu.VMEM((tm + FFN_HALO, tn), F32),
            pltpu.VMEM((tm + FFN_HALO, tn), F32),
        ],
        compiler_params=_compiler_params(("arbitrary", "arbitrary")),
        name="conv_ffn",
    )(h, h, norm_g, w_up, w_up, conv_w, conv_w, conv_b, conv_b, w_down, final_g)


def kernel(x, mem, mix_norm_g, mem_norm_g, w_in, gate_b, conv_w, conv_b, conv_ln_g, conv_ln_b, w_conv_out,
           rel_bias, w_attn_out, w_mem_kv, w_mem_out, w_o, ffn_norm_g, w_up, ffn_conv_w, ffn_conv_b, w_down,
           final_norm_g):
    batch, seq, d = x.shape
    n_mem = mem.shape[1]
    assert (d, n_mem) == (D_MODEL, N_MEM) and seq % ATTN_BLOCK == 0

    def row(p):
        return p.reshape(p.shape[0], 1, p.shape[1])

    w_in_b, w_conv_out_b, w_attn_out_b = w_in.astype(BF16), w_conv_out.astype(BF16), w_attn_out.astype(BF16)
    w_mem_kv_b, w_mem_out_b, w_o_b = w_mem_kv.astype(BF16), w_mem_out.astype(BF16), w_o.astype(BF16)
    w_up_b, w_down_b = w_up.astype(BF16), w_down.astype(BF16)
    mix_g, mem_g, ffn_g = row(mix_norm_g), row(mem_norm_g), row(ffn_norm_g)
    gate_b3, conv_b3, ln_g3, ln_b3, ffn_cb3 = row(gate_b), row(conv_b), row(conv_ln_g), row(conv_ln_b), row(ffn_conv_b)
    final_g = final_norm_g.reshape(1, D_MODEL)
    bias_table = _attn_bias_table(rel_bias)

    h = x.reshape(batch * seq, d)
    mem2 = mem.reshape(batch * n_mem, d)
    for layer in range(DEPTH):
        proj = _norm_matmul(h, mix_g, w_in_b, layer, tm=1024, tn=1024)
        kv = _norm_matmul(mem2, mem_g, w_mem_kv_b, layer, tm=batch * n_mem, tn=1024)
        c = _conv_branch(proj, conv_w, conv_b3, ln_g3, ln_b3, layer, tm=512, seq=seq)
        a = _chunk_attention(proj, bias_table, layer, seq=seq, batch=batch)
        m = _mem_attention(proj, kv, tm=512, seq=seq, batch=batch)
        h = _merge(h, c, a, m, proj, gate_b3, w_conv_out_b, w_attn_out_b, w_mem_out_b, w_o_b, layer, tm=1024, tn=512)
        h = _ffn(h, ffn_g, w_up_b, ffn_conv_w, ffn_cb3, w_down_b, final_g, layer, tm=1024, tn=512, seq=seq,
                 final_norm=(layer == DEPTH - 1))
    return h.reshape(batch, seq, d)
```

```python
import functools

import jax
import jax.numpy as jnp
import numpy as np
from jax import lax
from jax.experimental import pallas as pl
from jax.experimental.pallas import tpu as pltpu

D_MODEL = 2048
DEPTH = 4
CHUNK = 64
N_MEM = 256
N_BRANCH = 3
D_CONV = D_MODEL // 2
CONV_WIDTH = 31
ATTN_HEAD_DIM = 64
D_ATTN = D_MODEL // 2
N_ATTN_HEADS = D_ATTN // ATTN_HEAD_DIM
LEFT_CHUNKS = 8
BAND = LEFT_CHUNKS + 1
MAX_REL = 256
N_MEM_HEADS = 4
D_MEM = D_MODEL // 2
MEM_HEAD_DIM = D_MEM // N_MEM_HEADS
D_FF = ((8 * D_MODEL // 3 + 255) // 256) * 256
FFN_CONV_WIDTH = 3
D_IN = 2 * D_CONV + 3 * D_ATTN + D_MEM + N_BRANCH * D_MODEL
EPS = 1e-6
NEG_INF = -1e30

COL_GLU_A = 0
COL_GLU_G = D_CONV
COL_Q = 2 * D_CONV
COL_K = COL_Q + D_ATTN
COL_V = COL_K + D_ATTN
COL_QMEM = COL_V + D_ATTN
COL_GATES = COL_QMEM + D_MEM

V7X_LANES = 128
V7X_SCOPED_VMEM_BYTES = 60000 * 1024
BF16_SUBLANES = 16

F32 = jnp.float32
BF16 = jnp.bfloat16


def _compiler_params(semantics):
    return pltpu.CompilerParams(dimension_semantics=semantics, vmem_limit_bytes=V7X_SCOPED_VMEM_BYTES)


def _rms(x, g):
    ms = jnp.mean(x * x, axis=-1, keepdims=True)
    return (x * lax.rsqrt(ms + EPS)) * g


def _norm_matmul_kernel(x_ref, g_ref, w_ref, o_ref, xn_ref):
    @pl.when(pl.program_id(1) == 0)
    def _():
        xn_ref[...] = _rms(x_ref[...], g_ref[...]).astype(BF16)

    o_ref[...] = jnp.dot(xn_ref[...], w_ref[...], preferred_element_type=F32).astype(o_ref.dtype)


def _norm_matmul(x, g, w, layer, *, tm, tn):
    t, d = x.shape
    n = w.shape[-1]
    return pl.pallas_call(
        _norm_matmul_kernel,
        grid=(t // tm, n // tn),
        in_specs=[
            pl.BlockSpec((tm, d), lambda i, j: (i, 0)),
            pl.BlockSpec((None, 1, d), lambda i, j: (layer, 0, 0)),
            pl.BlockSpec((None, d, tn), lambda i, j: (layer, 0, j)),
        ],
        out_specs=pl.BlockSpec((tm, tn), lambda i, j: (i, j)),
        out_shape=jax.ShapeDtypeStruct((t, n), BF16),
        scratch_shapes=[pltpu.VMEM((tm, d), BF16)],
        compiler_params=_compiler_params(("arbitrary", "arbitrary")),
        name="norm_matmul",
    )(x, g, w)


F32_SUBLANES = 8
CONV_HALO = 32
CONV_ROWS = 64
CONV_BASE = CONV_HALO - (CONV_WIDTH - 1)
CONV_SHIFT_ROWS = CONV_HALO - F32_SUBLANES


def _conv_kernel(a_ref, gt_ref, ah_ref, gh_ref, cw_ref, cb_ref, lg_ref, lb_ref, o_ref, hbuf_ref, hs_ref, y_ref,
                 *, tiles_per_seq):
    tm = a_ref.shape[0]
    first = (pl.program_id(0) % tiles_per_seq) == 0

    def glu(a, g):
        return a.astype(F32) * jax.nn.sigmoid(g.astype(F32))

    halo = glu(ah_ref[...], gh_ref[...])
    hbuf_ref[0:CONV_HALO, :] = jnp.where(first, 0.0, halo)
    hbuf_ref[CONV_HALO:, :] = glu(a_ref[...], gt_ref[...])

    for s in range(1, F32_SUBLANES):
        hs_ref[s - 1] = hbuf_ref[pl.ds(s, tm + CONV_SHIFT_ROWS), :]

    n_lane_blocks = D_CONV // V7X_LANES

    def step(i, carry):
        r0 = pl.multiple_of((i // n_lane_blocks) * CONV_ROWS, CONV_ROWS)
        lanes = pl.ds(pl.multiple_of((i % n_lane_blocks) * V7X_LANES, V7X_LANES), V7X_LANES)
        acc = jnp.broadcast_to(cb_ref[:, lanes], (CONV_ROWS, V7X_LANES))
        for s in range(F32_SUBLANES):
            taps = [k for k in range(CONV_WIDTH) if (CONV_BASE + k) % F32_SUBLANES == s]
            span = CONV_ROWS + max(CONV_BASE + k - s for k in taps)
            if s == 0:
                xs = hbuf_ref[pl.ds(r0, span), lanes]
            else:
                xs = hs_ref[s - 1, pl.ds(r0, span), lanes]
            for k in taps:
                a = CONV_BASE + k - s
                acc = acc + cw_ref[k:k + 1, lanes] * xs[a:a + CONV_ROWS, :]
        y_ref[pl.ds(r0, CONV_ROWS), lanes] = acc
        return carry

    lax.fori_loop(0, (tm // CONV_ROWS) * n_lane_blocks, step, 0, unroll=2)

    y = y_ref[...]
    mu = jnp.mean(y, axis=-1, keepdims=True)
    yc = y - mu
    var = jnp.mean(yc * yc, axis=-1, keepdims=True)
    z = yc * lax.rsqrt(var + EPS) * lg_ref[...] + lb_ref[...]
    o_ref[...] = (z * jax.nn.sigmoid(z)).astype(o_ref.dtype)


def _conv_branch(proj, conv_w, conv_b, ln_g, ln_b, layer, *, tm, seq):
    t = proj.shape[0]
    halo_blocks = tm // CONV_HALO
    a_blk = COL_GLU_A // D_CONV
    g_blk = COL_GLU_G // D_CONV

    def halo_idx(col):
        return lambda i: (jnp.maximum(i * halo_blocks - 1, 0), col)

    return pl.pallas_call(
        functools.partial(_conv_kernel, tiles_per_seq=seq // tm),
        grid=(t // tm,),
        in_specs=[
            pl.BlockSpec((tm, D_CONV), lambda i: (i, a_blk)),
            pl.BlockSpec((tm, D_CONV), lambda i: (i, g_blk)),
            pl.BlockSpec((CONV_HALO, D_CONV), halo_idx(a_blk)),
            pl.BlockSpec((CONV_HALO, D_CONV), halo_idx(g_blk)),
            pl.BlockSpec((None, CONV_WIDTH, D_CONV), lambda i: (layer, 0, 0)),
            pl.BlockSpec((None, 1, D_CONV), lambda i: (layer, 0, 0)),
            pl.BlockSpec((None, 1, D_CONV), lambda i: (layer, 0, 0)),
            pl.BlockSpec((None, 1, D_CONV), lambda i: (layer, 0, 0)),
        ],
        out_specs=pl.BlockSpec((tm, D_CONV), lambda i: (i, 0)),
        out_shape=jax.ShapeDtypeStruct((t, D_CONV), BF16),
        scratch_shapes=[
            pltpu.VMEM((tm + CONV_HALO, D_CONV), F32),
            pltpu.VMEM((F32_SUBLANES - 1, tm + CONV_SHIFT_ROWS, D_CONV), F32),
            pltpu.VMEM((tm, D_CONV), F32),
        ],
        compiler_params=_compiler_params(("arbitrary",)),
        name="conv_branch",
    )(proj, proj, proj, proj, conv_w, conv_b, ln_g, ln_b)


ATTN_BLOCK = LEFT_CHUNKS * CHUNK
ATTN_SUB = 4 * CHUNK
ATTN_WIN = ATTN_SUB + LEFT_CHUNKS * CHUNK
HEADS_PER_STEP = V7X_LANES // ATTN_HEAD_DIM


def _attn_bias_table(rel_bias):
    period = ATTN_SUB + ATTN_WIN - 1
    m = np.arange(period)
    dist = (ATTN_WIN - 1 - m) % period - (ATTN_WIN - 1) + LEFT_CHUNKS * CHUNK
    gp = rel_bias.astype(F32)[:, :, np.clip(dist, -MAX_REL, MAX_REL) + MAX_REL]
    flat = jnp.tile(gp, (1, 1, ATTN_SUB))[:, :, :ATTN_SUB * (period - 1)]
    toeplitz = flat.reshape(flat.shape[0], flat.shape[1], ATTN_SUB, period - 1)[:, :, :, :ATTN_WIN]
    r = np.arange(ATTN_SUB)[:, None]
    km = np.arange(ATTN_WIN)[None, :] - (r // CHUNK) * CHUNK
    in_band = (km >= 0) & (km < BAND * CHUNK)
    table = jnp.where(in_band[None, None], toeplitz, NEG_INF)
    return table.reshape(table.shape[0], N_ATTN_HEADS // HEADS_PER_STEP, HEADS_PER_STEP * ATTN_SUB, ATTN_WIN)


def _attn_kernel(q_ref, km_ref, kh_ref, vm_ref, vh_ref, bias_ref, o_ref, kbuf_ref, vbuf_ref):
    first = (pl.program_id(2) == 0)
    kbuf_ref[0:ATTN_BLOCK, :] = kh_ref[...]
    kbuf_ref[ATTN_BLOCK:, :] = km_ref[...]
    vbuf_ref[0:ATTN_BLOCK, :] = vh_ref[...]
    vbuf_ref[ATTN_BLOCK:, :] = vm_ref[...]

    lane = lax.broadcasted_iota(jnp.int32, (1, V7X_LANES), 1)
    col = lax.broadcasted_iota(jnp.int32, (1, ATTN_WIN), 1)
    scale = ATTN_HEAD_DIM ** -0.5

    n_sub = ATTN_BLOCK // ATTN_SUB
    chains = [(j, h) for j in range(n_sub) for h in range(HEADS_PER_STEP)]

    def window(j):
        return slice(j * ATTN_SUB, j * ATTN_SUB + ATTN_WIN)

    def compute(mask_left):
        scores = []
        for j, h in chains:
            q = q_ref[j * ATTN_SUB:(j + 1) * ATTN_SUB, :] * scale
            qh = jnp.where((lane >= h * ATTN_HEAD_DIM) & (lane < (h + 1) * ATTN_HEAD_DIM), q, jnp.zeros_like(q))
            scores.append(lax.dot_general(qh, kbuf_ref[window(j), :], (((1,), (1,)), ((), ())),
                                          preferred_element_type=F32))
        probs, sums = [], []
        for (j, h), s in zip(chains, scores):
            s = s + bias_ref[h * ATTN_SUB:(h + 1) * ATTN_SUB, :]
            if mask_left:
                s = jnp.where(col < ATTN_BLOCK - j * ATTN_SUB, NEG_INF, s)
            p = jnp.exp(s - jnp.max(s, axis=-1, keepdims=True))
            sums.append(jnp.sum(p, axis=-1, keepdims=True))
            probs.append(p.astype(BF16))
        outs = {}
        for (j, h), p, l in zip(chains, probs, sums):
            outs[j, h] = jnp.dot(p, vbuf_ref[window(j), :], preferred_element_type=F32) / l
        for j in range(n_sub):
            out = outs[j, 0]
            for h in range(1, HEADS_PER_STEP):
                out = jnp.where(lane >= h * ATTN_HEAD_DIM, outs[j, h], out)
            o_ref[j * ATTN_SUB:(j + 1) * ATTN_SUB, :] = out.astype(o_ref.dtype)

    @pl.when(first)
    def _():
        compute(True)

    @pl.when(jnp.logical_not(first))
    def _():
        compute(False)


def _chunk_attention(proj, bias_table, layer, *, seq, batch):
    t = proj.shape[0]
    bps = seq // ATTN_BLOCK
    q_blk, k_blk, v_blk = COL_Q // V7X_LANES, COL_K // V7X_LANES, COL_V // V7X_LANES

    def main_idx(col0):
        return lambda hp, b, i: (b * bps + i, col0 + hp)

    def halo_idx(col0):
        return lambda hp, b, i: (b * bps + jnp.maximum(i - 1, 0), col0 + hp)

    blk = (ATTN_BLOCK, V7X_LANES)
    return pl.pallas_call(
        _attn_kernel,
        grid=(N_ATTN_HEADS // HEADS_PER_STEP, batch, bps),
        in_specs=[
            pl.BlockSpec(blk, main_idx(q_blk)),
            pl.BlockSpec(blk, main_idx(k_blk)),
            pl.BlockSpec(blk, halo_idx(k_blk)),
            pl.BlockSpec(blk, main_idx(v_blk)),
            pl.BlockSpec(blk, halo_idx(v_blk)),
            pl.BlockSpec((None, None, HEADS_PER_STEP * ATTN_SUB, ATTN_WIN), lambda hp, b, i: (layer, hp, 0, 0)),
        ],
        out_specs=pl.BlockSpec(blk, lambda hp, b, i: (b * bps + i, hp)),
        out_shape=jax.ShapeDtypeStruct((t, D_ATTN), BF16),
        scratch_shapes=[pltpu.VMEM((2 * ATTN_BLOCK, V7X_LANES), BF16), pltpu.VMEM((2 * ATTN_BLOCK, V7X_LANES), BF16)],
        compiler_params=_compiler_params(("arbitrary", "arbitrary", "arbitrary")),
        name="chunk_attention",
    )(proj, proj, proj, proj, proj, bias_table)


def _mem_attn_kernel(q_ref, kv_ref, o_ref):
    scale = MEM_HEAD_DIM ** -0.5
    for h in range(N_MEM_HEADS):
        lanes = slice(h * MEM_HEAD_DIM, (h + 1) * MEM_HEAD_DIM)
        q = q_ref[:, lanes] * scale
        k = kv_ref[:, lanes]
        v = kv_ref[:, D_MEM + h * MEM_HEAD_DIM:D_MEM + (h + 1) * MEM_HEAD_DIM]
        s = lax.dot_general(q, k, (((1,), (1,)), ((), ())), preferred_element_type=F32)
        m = jnp.max(s, axis=-1, keepdims=True)
        p = jnp.exp(s - m)
        l = jnp.sum(p, axis=-1, keepdims=True)
        o = jnp.dot(p.astype(BF16), v, preferred_element_type=F32)
        o_ref[:, lanes] = (o / l).astype(o_ref.dtype)


def _mem_attention(proj, kv, *, tm, seq, batch):
    t = proj.shape[0]
    tiles = seq // tm
    return pl.pallas_call(
        _mem_attn_kernel,
        grid=(batch, tiles),
        in_specs=[
            pl.BlockSpec((tm, D_MEM), lambda b, i: (b * tiles + i, COL_QMEM // D_MEM)),
            pl.BlockSpec((N_MEM, 2 * D_MEM), lambda b, i: (b, 0)),
        ],
        out_specs=pl.BlockSpec((tm, D_MEM), lambda b, i: (b * tiles + i, 0)),
        out_shape=jax.ShapeDtypeStruct((t, D_MEM), BF16),
        compiler_params=_compiler_params(("arbitrary", "arbitrary")),
        name="mem_attention",
    )(proj, kv)


def _merge_kernel(h_ref, c_ref, a_ref, m_ref, g0_ref, g1_ref, g2_ref, b0_ref, b1_ref, b2_ref,
                  wc_ref, wa_ref, wm_ref, wo_ref, o_ref):
    j = pl.program_id(1)

    def gated(x_ref, w_ref, g_ref, b_ref):
        y = jnp.dot(x_ref[...], w_ref[...], preferred_element_type=F32)
        return jax.nn.sigmoid(g_ref[...].astype(F32) + b_ref[...]) * y

    @pl.when(j == 0)
    def _():
        o_ref[...] = h_ref[...]

    merged = (gated(c_ref, wc_ref, g0_ref, b0_ref) + gated(a_ref, wa_ref, g1_ref, b1_ref)
              + gated(m_ref, wm_ref, g2_ref, b2_ref))
    o_ref[...] += jnp.dot(merged.astype(BF16), wo_ref[...], preferred_element_type=F32)


def _merge(h, c, a, m, proj, gate_b, w_conv_out, w_attn_out, w_mem_out, w_o, layer, *, tm, tn):
    t = h.shape[0]
    nj = D_MODEL // tn
    gate_blk = COL_GATES // tn

    def gate_idx(branch):
        return lambda i, j: (i, gate_blk + branch * nj + j)

    def bias_idx(branch):
        return lambda i, j: (layer, 0, branch * nj + j)

    row_spec = pl.BlockSpec((tm, D_CONV), lambda i, j: (i, 0))
    w_spec = pl.BlockSpec((None, D_CONV, tn), lambda i, j: (layer, 0, j))
    return pl.pallas_call(
        _merge_kernel,
        grid=(t // tm, nj),
        in_specs=[
            pl.BlockSpec((tm, D_MODEL), lambda i, j: (i, 0)),
            row_spec, row_spec, row_spec,
            pl.BlockSpec((tm, tn), gate_idx(0)),
            pl.BlockSpec((tm, tn), gate_idx(1)),
            pl.BlockSpec((tm, tn), gate_idx(2)),
            pl.BlockSpec((None, 1, tn), bias_idx(0)),
            pl.BlockSpec((None, 1, tn), bias_idx(1)),
            pl.BlockSpec((None, 1, tn), bias_idx(2)),
            w_spec, w_spec, w_spec,
            pl.BlockSpec((None, tn, D_MODEL), lambda i, j: (layer, j, 0)),
        ],
        out_specs=pl.BlockSpec((tm, D_MODEL), lambda i, j: (i, 0), pipeline_mode=pl.Buffered(1)),
        out_shape=jax.ShapeDtypeStruct((t, D_MODEL), F32),
        compiler_params=_compiler_params(("arbitrary", "arbitrary")),
        name="merge",
    )(h, c, a, m, proj, proj, proj, gate_b, gate_b, gate_b, w_conv_out, w_attn_out, w_mem_out, w_o)


FFN_HALO = BF16_SUBLANES


def _ffn_kernel(h_ref, halo_ref, g_ref, wv_ref, wg_ref, cwv_ref, cwg_ref, cbv_ref, cbg_ref, wd_ref, fg_ref,
                o_ref, hn_ref, uv_ref, ug_ref, *, tiles_per_seq, final_norm):
    tm = h_ref.shape[0]
    i, j = pl.program_id(0), pl.program_id(1)

    @pl.when(j == 0)
    def _():
        first = (i % tiles_per_seq) == 0
        halo = _rms(halo_ref[...], g_ref[...])
        hn_ref[0:FFN_HALO, :] = jnp.where(first, 0.0, halo).astype(BF16)
        x = h_ref[...]
        o_ref[...] = x
        hn_ref[FFN_HALO:, :] = _rms(x, g_ref[...]).astype(BF16)

    hn = hn_ref[...]
    uv_ref[...] = jnp.dot(hn, wv_ref[...], preferred_element_type=F32)
    ug_ref[...] = jnp.dot(hn, wg_ref[...], preferred_element_type=F32)

    def dwconv(u_ref, cw_ref, cb_ref):
        acc = cb_ref[...]
        for k in range(FFN_CONV_WIDTH):
            acc = acc + cw_ref[k:k + 1, :] * u_ref[pl.ds(FFN_HALO - (FFN_CONV_WIDTH - 1) + k, tm), :]
        return acc

    val = dwconv(uv_ref, cwv_ref, cbv_ref)
    gt = dwconv(ug_ref, cwg_ref, cbg_ref)
    act = (gt * jax.nn.sigmoid(gt) * val).astype(BF16)
    o_ref[...] += jnp.dot(act, wd_ref[...], preferred_element_type=F32)

    if final_norm:
        @pl.when(j == pl.num_programs(1) - 1)
        def _():
            o_ref[...] = _rms(o_ref[...], fg_ref[...])


def _ffn(h, norm_g, w_up, conv_w, conv_b, w_down, final_g, layer, *, tm, tn, seq, final_norm):
    t = h.shape[0]
    nj = D_FF // tn
    halo_blocks = tm // FFN_HALO
    return pl.pallas_call(
        functools.partial(_ffn_kernel, tiles_per_seq=seq // tm, final_norm=final_norm),
        grid=(t // tm, nj),
        in_specs=[
            pl.BlockSpec((tm, D_MODEL), lambda i, j: (i, 0)),
            pl.BlockSpec((FFN_HALO, D_MODEL), lambda i, j: (jnp.maximum(i * halo_blocks - 1, 0), 0)),
            pl.BlockSpec((None, 1, D_MODEL), lambda i, j: (layer, 0, 0)),
            pl.BlockSpec((None, D_MODEL, tn), lambda i, j: (layer, 0, j)),
            pl.BlockSpec((None, D_MODEL, tn), lambda i, j: (layer, 0, nj + j)),
            pl.BlockSpec((None, FFN_CONV_WIDTH, tn), lambda i, j: (layer, 0, j)),
            pl.BlockSpec((None, FFN_CONV_WIDTH, tn), lambda i, j: (layer, 0, nj + j)),
            pl.BlockSpec((None, 1, tn), lambda i, j: (layer, 0, j)),
            pl.BlockSpec((None, 1, tn), lambda i, j: (layer, 0, nj + j)),
            pl.BlockSpec((None, tn, D_MODEL), lambda i, j: (layer, j, 0)),
            pl.BlockSpec((1, D_MODEL), lambda i, j: (0, 0)),
        ],
        out_specs=pl.BlockSpec((tm, D_MODEL), lambda i, j: (i, 0), pipeline_mode=pl.Buffered(1)),
        out_shape=jax.ShapeDtypeStruct((t, D_MODEL), F32),
        scratch_shapes=[
            pltpu.VMEM((tm + FFN_HALO, D_MODEL), BF16),
            pltpu.VMEM((tm + FFN_HALO, tn), F32),
            pltpu.VMEM((tm + FFN_HALO, tn), F32),
        ],
        compiler_params=_compiler_params(("arbitrary", "arbitrary")),
        name="conv_ffn",
    )(h, h, norm_g, w_up, w_up, conv_w, conv_w, conv_b, conv_b, w_down, final_g)


def kernel(x, mem, mix_norm_g, mem_norm_g, w_in, gate_b, conv_w, conv_b, conv_ln_g, conv_ln_b, w_conv_out,
           rel_bias, w_attn_out, w_mem_kv, w_mem_out, w_o, ffn_norm_g, w_up, ffn_conv_w, ffn_conv_b, w_down,
           final_norm_g):
    batch, seq, d = x.shape
    n_mem = mem.shape[1]
    assert (d, n_mem) == (D_MODEL, N_MEM) and seq % ATTN_BLOCK == 0

    def row(p):
        return p.reshape(p.shape[0], 1, p.shape[1])

    w_in_b, w_conv_out_b, w_attn_out_b = w_in.astype(BF16), w_conv_out.astype(BF16), w_attn_out.astype(BF16)
    w_mem_kv_b, w_mem_out_b, w_o_b = w_mem_kv.astype(BF16), w_mem_out.astype(BF16), w_o.astype(BF16)
    w_up_b, w_down_b = w_up.astype(BF16), w_down.astype(BF16)
    mix_g, mem_g, ffn_g = row(mix_norm_g), row(mem_norm_g), row(ffn_norm_g)
    gate_b3, conv_b3, ln_g3, ln_b3, ffn_cb3 = row(gate_b), row(conv_b), row(conv_ln_g), row(conv_ln_b), row(ffn_conv_b)
    final_g = final_norm_g.reshape(1, D_MODEL)
    bias_table = _attn_bias_table(rel_bias)

    h = x.reshape(batch * seq, d)
    mem2 = mem.reshape(batch * n_mem, d)
    for layer in range(DEPTH):
        proj = _norm_matmul(h, mix_g, w_in_b, layer, tm=1024, tn=1024)
        kv = _norm_matmul(mem2, mem_g, w_mem_kv_b, layer, tm=batch * n_mem, tn=1024)
        c = _conv_branch(proj, conv_w, conv_b3, ln_g3, ln_b3, layer, tm=512, seq=seq)
        a = _chunk_attention(proj, bias_table, layer, seq=seq, batch=batch)
        m = _mem_attention(proj, kv, tm=512, seq=seq, batch=batch)
        h = _merge(h, c, a, m, proj, gate_b3, w_conv_out_b, w_attn_out_b, w_mem_out_b, w_o_b, layer, tm=1024, tn=512)
        h = _ffn(h, ffn_g, w_up_b, ffn_conv_w, ffn_cb3, w_down_b, final_g, layer, tm=1024, tn=512, seq=seq,
                 final_norm=(layer == DEPTH - 1))
    return h.reshape(batch, seq, d)
```

```python
import functools

import jax
import jax.numpy as jnp
import numpy as np
from jax import lax
from jax.experimental import pallas as pl
from jax.experimental.pallas import tpu as pltpu

D_MODEL = 2048
DEPTH = 4
CHUNK = 64
N_MEM = 256
N_BRANCH = 3
D_CONV = D_MODEL // 2
CONV_WIDTH = 31
ATTN_HEAD_DIM = 64
D_ATTN = D_MODEL // 2
N_ATTN_HEADS = D_ATTN // ATTN_HEAD_DIM
LEFT_CHUNKS = 8
BAND = LEFT_CHUNKS + 1
MAX_REL = 256
N_MEM_HEADS = 4
D_MEM = D_MODEL // 2
MEM_HEAD_DIM = D_MEM // N_MEM_HEADS
D_FF = ((8 * D_MODEL // 3 + 255) // 256) * 256
FFN_CONV_WIDTH = 3
D_IN = 2 * D_CONV + 3 * D_ATTN + D_MEM + N_BRANCH * D_MODEL
EPS = 1e-6
NEG_INF = -1e30

COL_GLU_A = 0
COL_GLU_G = D_CONV
COL_Q = 2 * D_CONV
COL_K = COL_Q + D_ATTN
COL_V = COL_K + D_ATTN
COL_QMEM = COL_V + D_ATTN
COL_GATES = COL_QMEM + D_MEM

V7X_LANES = 128
V7X_SCOPED_VMEM_BYTES = 60000 * 1024
BF16_SUBLANES = 16

F32 = jnp.float32
BF16 = jnp.bfloat16


def _compiler_params(semantics):
    return pltpu.CompilerParams(dimension_semantics=semantics, vmem_limit_bytes=V7X_SCOPED_VMEM_BYTES)


def _rms(x, g):
    ms = jnp.mean(x * x, axis=-1, keepdims=True)
    return (x * lax.rsqrt(ms + EPS)) * g


def _norm_matmul_kernel(x_ref, g_ref, w_ref, o_ref, xn_ref):
    @pl.when(pl.program_id(1) == 0)
    def _():
        xn_ref[...] = _rms(x_ref[...], g_ref[...]).astype(BF16)

    o_ref[...] = jnp.dot(xn_ref[...], w_ref[...], preferred_element_type=F32).astype(o_ref.dtype)


def _norm_matmul(x, g, w, layer, *, tm, tn):
    t, d = x.shape
    n = w.shape[-1]
    return pl.pallas_call(
        _norm_matmul_kernel,
        grid=(t // tm, n // tn),
        in_specs=[
            pl.BlockSpec((tm, d), lambda i, j: (i, 0)),
            pl.BlockSpec((None, 1, d), lambda i, j: (layer, 0, 0)),
            pl.BlockSpec((None, d, tn), lambda i, j: (layer, 0, j)),
        ],
        out_specs=pl.BlockSpec((tm, tn), lambda i, j: (i, j)),
        out_shape=jax.ShapeDtypeStruct((t, n), BF16),
        scratch_shapes=[pltpu.VMEM((tm, d), BF16)],
        compiler_params=_compiler_params(("arbitrary", "arbitrary")),
        name="norm_matmul",
    )(x, g, w)


F32_SUBLANES = 8
CONV_HALO = 32
CONV_ROWS = 64
CONV_BASE = CONV_HALO - (CONV_WIDTH - 1)
CONV_SHIFT_ROWS = CONV_HALO - F32_SUBLANES


def _conv_kernel(a_ref, gt_ref, ah_ref, gh_ref, cw_ref, cb_ref, lg_ref, lb_ref, o_ref, hbuf_ref, hs_ref, y_ref,
                 *, tiles_per_seq):
    tm = a_ref.shape[0]
    first = (pl.program_id(0) % tiles_per_seq) == 0

    def glu(a, g):
        return a.astype(F32) * jax.nn.sigmoid(g.astype(F32))

    halo = glu(ah_ref[...], gh_ref[...])
    hbuf_ref[0:CONV_HALO, :] = jnp.where(first, 0.0, halo)
    hbuf_ref[CONV_HALO:, :] = glu(a_ref[...], gt_ref[...])

    for s in range(1, F32_SUBLANES):
        hs_ref[s - 1] = hbuf_ref[pl.ds(s, tm + CONV_SHIFT_ROWS), :]

    n_lane_blocks = D_CONV // V7X_LANES

    def step(i, carry):
        r0 = pl.multiple_of((i // n_lane_blocks) * CONV_ROWS, CONV_ROWS)
        lanes = pl.ds(pl.multiple_of((i % n_lane_blocks) * V7X_LANES, V7X_LANES), V7X_LANES)
        acc = jnp.broadcast_to(cb_ref[:, lanes], (CONV_ROWS, V7X_LANES))
        for s in range(F32_SUBLANES):
            taps = [k for k in range(CONV_WIDTH) if (CONV_BASE + k) % F32_SUBLANES == s]
            span = CONV_ROWS + max(CONV_BASE + k - s for k in taps)
            if s == 0:
                xs = hbuf_ref[pl.ds(r0, span), lanes]
            else:
                xs = hs_ref[s - 1, pl.ds(r0, span), lanes]
            for k in taps:
                a = CONV_BASE + k - s
                acc = acc + cw_ref[k:k + 1, lanes] * xs[a:a + CONV_ROWS, :]
        y_ref[pl.ds(r0, CONV_ROWS), lanes] = acc
        return carry

    lax.fori_loop(0, (tm // CONV_ROWS) * n_lane_blocks, step, 0, unroll=4)

    y = y_ref[...]
    mu = jnp.mean(y, axis=-1, keepdims=True)
    yc = y - mu
    var = jnp.mean(yc * yc, axis=-1, keepdims=True)
    z = yc * lax.rsqrt(var + EPS) * lg_ref[...] + lb_ref[...]
    o_ref[...] = (z * jax.nn.sigmoid(z)).astype(o_ref.dtype)


def _conv_branch(proj, conv_w, conv_b, ln_g, ln_b, layer, *, tm, seq):
    t = proj.shape[0]
    halo_blocks = tm // CONV_HALO
    a_blk = COL_GLU_A // D_CONV
    g_blk = COL_GLU_G // D_CONV

    def halo_idx(col):
        return lambda i: (jnp.maximum(i * halo_blocks - 1, 0), col)

    return pl.pallas_call(
        functools.partial(_conv_kernel, tiles_per_seq=seq // tm),
        grid=(t // tm,),
        in_specs=[
            pl.BlockSpec((tm, D_CONV), lambda i: (i, a_blk)),
            pl.BlockSpec((tm, D_CONV), lambda i: (i, g_blk)),
            pl.BlockSpec((CONV_HALO, D_CONV), halo_idx(a_blk)),
            pl.BlockSpec((CONV_HALO, D_CONV), halo_idx(g_blk)),
            pl.BlockSpec((None, CONV_WIDTH, D_CONV), lambda i: (layer, 0, 0)),
            pl.BlockSpec((None, 1, D_CONV), lambda i: (layer, 0, 0)),
            pl.BlockSpec((None, 1, D_CONV), lambda i: (layer, 0, 0)),
            pl.BlockSpec((None, 1, D_CONV), lambda i: (layer, 0, 0)),
        ],
        out_specs=pl.BlockSpec((tm, D_CONV), lambda i: (i, 0)),
        out_shape=jax.ShapeDtypeStruct((t, D_CONV), BF16),
        scratch_shapes=[
            pltpu.VMEM((tm + CONV_HALO, D_CONV), F32),
            pltpu.VMEM((F32_SUBLANES - 1, tm + CONV_SHIFT_ROWS, D_CONV), F32),
            pltpu.VMEM((tm, D_CONV), F32),
        ],
        compiler_params=_compiler_params(("arbitrary",)),
        name="conv_branch",
    )(proj, proj, proj, proj, conv_w, conv_b, ln_g, ln_b)


ATTN_BLOCK = LEFT_CHUNKS * CHUNK
ATTN_SUB = 4 * CHUNK
ATTN_WIN = ATTN_SUB + LEFT_CHUNKS * CHUNK
HEADS_PER_STEP = V7X_LANES // ATTN_HEAD_DIM


def _attn_bias_table(rel_bias):
    extra = ATTN_SUB - CHUNK
    width = ATTN_WIN + extra
    period = CHUNK + width - 1
    m = np.arange(period)
    m_signed = np.where(m < width, m, m - period)
    dist = LEFT_CHUNKS * CHUNK + extra - m_signed
    gp = rel_bias.astype(F32)[:, :, np.clip(dist, -MAX_REL, MAX_REL) + MAX_REL]
    flat = jnp.tile(gp, (1, 1, CHUNK))[:, :, :CHUNK * (period - 1)]
    base = flat.reshape(flat.shape[0], flat.shape[1], CHUNK, period - 1)
    km = np.arange(ATTN_WIN)[None, :] - np.arange(ATTN_SUB // CHUNK)[:, None] * CHUNK
    in_band = (km >= 0) & (km < BAND * CHUNK)
    chunks = [jnp.where(in_band[cq][None, None, None, :],
                        base[:, :, :, extra - cq * CHUNK:extra - cq * CHUNK + ATTN_WIN], NEG_INF)
              for cq in range(ATTN_SUB // CHUNK)]
    table = jnp.concatenate(chunks, axis=2)
    return table.reshape(table.shape[0], N_ATTN_HEADS // HEADS_PER_STEP, HEADS_PER_STEP * ATTN_SUB, ATTN_WIN)


def _attn_kernel(q_ref, km_ref, kh_ref, vm_ref, vh_ref, bias_ref, o_ref, kbuf_ref, vbuf_ref):
    first = (pl.program_id(2) == 0)
    kbuf_ref[0:ATTN_BLOCK, :] = kh_ref[...]
    kbuf_ref[ATTN_BLOCK:, :] = km_ref[...]
    vbuf_ref[0:ATTN_BLOCK, :] = vh_ref[...]
    vbuf_ref[ATTN_BLOCK:, :] = vm_ref[...]

    lane = lax.broadcasted_iota(jnp.int32, (1, V7X_LANES), 1)
    col = lax.broadcasted_iota(jnp.int32, (1, ATTN_WIN), 1)
    scale = ATTN_HEAD_DIM ** -0.5

    n_sub = ATTN_BLOCK // ATTN_SUB
    chains = [(j, h) for j in range(n_sub) for h in range(HEADS_PER_STEP)]

    def window(j):
        return slice(j * ATTN_SUB, j * ATTN_SUB + ATTN_WIN)

    def compute(mask_left):
        scores = []
        for j, h in chains:
            q = q_ref[j * ATTN_SUB:(j + 1) * ATTN_SUB, :] * scale
            qh = jnp.where((lane >= h * ATTN_HEAD_DIM) & (lane < (h + 1) * ATTN_HEAD_DIM), q, jnp.zeros_like(q))
            scores.append(lax.dot_general(qh, kbuf_ref[window(j), :], (((1,), (1,)), ((), ())),
                                          preferred_element_type=F32))
        probs, sums = [], []
        for (j, h), s in zip(chains, scores):
            s = s + bias_ref[h * ATTN_SUB:(h + 1) * ATTN_SUB, :]
            if mask_left:
                s = jnp.where(col < ATTN_BLOCK - j * ATTN_SUB, NEG_INF, s)
            p = jnp.exp(s - jnp.max(s, axis=-1, keepdims=True))
            sums.append(jnp.sum(p, axis=-1, keepdims=True))
            probs.append(p.astype(BF16))
        outs = {}
        for (j, h), p, l in zip(chains, probs, sums):
            outs[j, h] = jnp.dot(p, vbuf_ref[window(j), :], preferred_element_type=F32) / l
        for j in range(n_sub):
            out = outs[j, 0]
            for h in range(1, HEADS_PER_STEP):
                out = jnp.where(lane >= h * ATTN_HEAD_DIM, outs[j, h], out)
            o_ref[j * ATTN_SUB:(j + 1) * ATTN_SUB, :] = out.astype(o_ref.dtype)

    @pl.when(first)
    def _():
        compute(True)

    @pl.when(jnp.logical_not(first))
    def _():
        compute(False)


def _chunk_attention(proj, bias_table, layer, *, seq, batch):
    t = proj.shape[0]
    bps = seq // ATTN_BLOCK
    q_blk, k_blk, v_blk = COL_Q // V7X_LANES, COL_K // V7X_LANES, COL_V // V7X_LANES

    def main_idx(col0):
        return lambda hp, b, i: (b * bps + i, col0 + hp)

    def halo_idx(col0):
        return lambda hp, b, i: (b * bps + jnp.maximum(i - 1, 0), col0 + hp)

    blk = (ATTN_BLOCK, V7X_LANES)
    return pl.pallas_call(
        _attn_kernel,
        grid=(N_ATTN_HEADS // HEADS_PER_STEP, batch, bps),
        in_specs=[
            pl.BlockSpec(blk, main_idx(q_blk)),
            pl.BlockSpec(blk, main_idx(k_blk)),
            pl.BlockSpec(blk, halo_idx(k_blk)),
            pl.BlockSpec(blk, main_idx(v_blk)),
            pl.BlockSpec(blk, halo_idx(v_blk)),
            pl.BlockSpec((None, None, HEADS_PER_STEP * ATTN_SUB, ATTN_WIN), lambda hp, b, i: (layer, hp, 0, 0)),
        ],
        out_specs=pl.BlockSpec(blk, lambda hp, b, i: (b * bps + i, hp)),
        out_shape=jax.ShapeDtypeStruct((t, D_ATTN), BF16),
        scratch_shapes=[pltpu.VMEM((2 * ATTN_BLOCK, V7X_LANES), BF16), pltpu.VMEM((2 * ATTN_BLOCK, V7X_LANES), BF16)],
        compiler_params=_compiler_params(("arbitrary", "arbitrary", "arbitrary")),
        name="chunk_attention",
    )(proj, proj, proj, proj, proj, bias_table)


def _mem_attn_kernel(q_ref, kv_ref, o_ref):
    scale = MEM_HEAD_DIM ** -0.5
    for h in range(N_MEM_HEADS):
        lanes = slice(h * MEM_HEAD_DIM, (h + 1) * MEM_HEAD_DIM)
        q = q_ref[:, lanes] * scale
        k = kv_ref[:, lanes]
        v = kv_ref[:, D_MEM + h * MEM_HEAD_DIM:D_MEM + (h + 1) * MEM_HEAD_DIM]
        s = lax.dot_general(q, k, (((1,), (1,)), ((), ())), preferred_element_type=F32)
        m = jnp.max(s, axis=-1, keepdims=True)
        p = jnp.exp(s - m)
        l = jnp.sum(p, axis=-1, keepdims=True)
        o = jnp.dot(p.astype(BF16), v, preferred_element_type=F32)
        o_ref[:, lanes] = (o / l).astype(o_ref.dtype)


def _mem_attention(proj, kv, *, tm, seq, batch):
    t = proj.shape[0]
    tiles = seq // tm
    return pl.pallas_call(
        _mem_attn_kernel,
        grid=(batch, tiles),
        in_specs=[
            pl.BlockSpec((tm, D_MEM), lambda b, i: (b * tiles + i, COL_QMEM // D_MEM)),
            pl.BlockSpec((N_MEM, 2 * D_MEM), lambda b, i: (b, 0)),
        ],
        out_specs=pl.BlockSpec((tm, D_MEM), lambda b, i: (b * tiles + i, 0)),
        out_shape=jax.ShapeDtypeStruct((t, D_MEM), BF16),
        compiler_params=_compiler_params(("arbitrary", "arbitrary")),
        name="mem_attention",
    )(proj, kv)


def _merge_kernel(h_ref, c_ref, a_ref, m_ref, g0_ref, g1_ref, g2_ref, b0_ref, b1_ref, b2_ref,
                  wc_ref, wa_ref, wm_ref, wo_ref, o_ref):
    j = pl.program_id(1)

    def gated(x_ref, w_ref, g_ref, b_ref):
        y = jnp.dot(x_ref[...], w_ref[...], preferred_element_type=F32)
        return jax.nn.sigmoid(g_ref[...].astype(F32) + b_ref[...]) * y

    @pl.when(j == 0)
    def _():
        o_ref[...] = h_ref[...]

    merged = (gated(c_ref, wc_ref, g0_ref, b0_ref) + gated(a_ref, wa_ref, g1_ref, b1_ref)
              + gated(m_ref, wm_ref, g2_ref, b2_ref))
    o_ref[...] += jnp.dot(merged.astype(BF16), wo_ref[...], preferred_element_type=F32)


def _merge(h, c, a, m, proj, gate_b, w_conv_out, w_attn_out, w_mem_out, w_o, layer, *, tm, tn):
    t = h.shape[0]
    nj = D_MODEL // tn
    gate_blk = COL_GATES // tn

    def gate_idx(branch):
        return lambda i, j: (i, gate_blk + branch * nj + j)

    def bias_idx(branch):
        return lambda i, j: (layer, 0, branch * nj + j)

    row_spec = pl.BlockSpec((tm, D_CONV), lambda i, j: (i, 0))
    w_spec = pl.BlockSpec((None, D_CONV, tn), lambda i, j: (layer, 0, j))
    return pl.pallas_call(
        _merge_kernel,
        grid=(t // tm, nj),
        in_specs=[
            pl.BlockSpec((tm, D_MODEL), lambda i, j: (i, 0)),
            row_spec, row_spec, row_spec,
            pl.BlockSpec((tm, tn), gate_idx(0)),
            pl.BlockSpec((tm, tn), gate_idx(1)),
            pl.BlockSpec((tm, tn), gate_idx(2)),
            pl.BlockSpec((None, 1, tn), bias_idx(0)),
            pl.BlockSpec((None, 1, tn), bias_idx(1)),
            pl.BlockSpec((None, 1, tn), bias_idx(2)),
            w_spec, w_spec, w_spec,
            pl.BlockSpec((None, tn, D_MODEL), lambda i, j: (layer, j, 0)),
        ],
        out_specs=pl.BlockSpec((tm, D_MODEL), lambda i, j: (i, 0), pipeline_mode=pl.Buffered(1)),
        out_shape=jax.ShapeDtypeStruct((t, D_MODEL), F32),
        compiler_params=_compiler_params(("arbitrary", "arbitrary")),
        name="merge",
    )(h, c, a, m, proj, proj, proj, gate_b, gate_b, gate_b, w_conv_out, w_attn_out, w_mem_out, w_o)


FFN_HALO = BF16_SUBLANES
FFN_ROW_SPLITS = 2


def _ffn_kernel(h_ref, halo_ref, g_ref, wv_ref, wg_ref, cwv_ref, cwg_ref, cbv_ref, cbg_ref, wd_ref, fg_ref,
                o_ref, hn_ref, *u_refs, tiles_per_seq, final_norm):
    tm = h_ref.shape[0]
    rows = tm // FFN_ROW_SPLITS
    i, j = pl.program_id(0), pl.program_id(1)

    @pl.when(j == 0)
    def _():
        first = (i % tiles_per_seq) == 0
        halo = _rms(halo_ref[...], g_ref[...])
        hn_ref[0:FFN_HALO, :] = jnp.where(first, 0.0, halo).astype(BF16)
        x = h_ref[...]
        o_ref[...] = x
        hn_ref[FFN_HALO:, :] = _rms(x, g_ref[...]).astype(BF16)

    def up(s):
        hn = hn_ref[s * rows:(s + 1) * rows + FFN_HALO, :]
        u_refs[2 * s][...] = jnp.dot(hn, wv_ref[...], preferred_element_type=F32)
        u_refs[2 * s + 1][...] = jnp.dot(hn, wg_ref[...], preferred_element_type=F32)

    def dwconv(u_ref, cw_ref, cb_ref):
        acc = cb_ref[...]
        for k in range(FFN_CONV_WIDTH):
            acc = acc + cw_ref[k:k + 1, :] * u_ref[pl.ds(FFN_HALO - (FFN_CONV_WIDTH - 1) + k, rows), :]
        return acc

    def down(s):
        val = dwconv(u_refs[2 * s], cwv_ref, cbv_ref)
        gt = dwconv(u_refs[2 * s + 1], cwg_ref, cbg_ref)
        act = (gt * jax.nn.sigmoid(gt) * val).astype(BF16)
        o_ref[s * rows:(s + 1) * rows, :] += jnp.dot(act, wd_ref[...], preferred_element_type=F32)

    up(0)
    for s in range(FFN_ROW_SPLITS):
        if s + 1 < FFN_ROW_SPLITS:
            up(s + 1)
        down(s)

    if final_norm:
        @pl.when(j == pl.num_programs(1) - 1)
        def _():
            o_ref[...] = _rms(o_ref[...], fg_ref[...])


def _ffn(h, norm_g, w_up, conv_w, conv_b, w_down, final_g, layer, *, tm, tn, seq, final_norm):
    t = h.shape[0]
    nj = D_FF // tn
    halo_blocks = tm // FFN_HALO
    return pl.pallas_call(
        functools.partial(_ffn_kernel, tiles_per_seq=seq // tm, final_norm=final_norm),
        grid=(t // tm, nj),
        in_specs=[
            pl.BlockSpec((tm, D_MODEL), lambda i, j: (i, 0)),
            pl.BlockSpec((FFN_HALO, D_MODEL), lambda i, j: (jnp.maximum(i * halo_blocks - 1, 0), 0)),
            pl.BlockSpec((None, 1, D_MODEL), lambda i, j: (layer, 0, 0)),
            pl.BlockSpec((None, D_MODEL, tn), lambda i, j: (layer, 0, j)),
            pl.BlockSpec((None, D_MODEL, tn), lambda i, j: (layer, 0, nj + j)),
            pl.BlockSpec((None, FFN_CONV_WIDTH, tn), lambda i, j: (layer, 0, j)),
            pl.BlockSpec((None, FFN_CONV_WIDTH, tn), lambda i, j: (layer, 0, nj + j)),
            pl.BlockSpec((None, 1, tn), lambda i, j: (layer, 0, j)),
            pl.BlockSpec((None, 1, tn), lambda i, j: (layer, 0, nj + j)),
            pl.BlockSpec((None, tn, D_MODEL), lambda i, j: (layer, j, 0)),
            pl.BlockSpec((1, D_MODEL), lambda i, j: (0, 0)),
        ],
        out_specs=pl.BlockSpec((tm, D_MODEL), lambda i, j: (i, 0), pipeline_mode=pl.Buffered(1)),
        out_shape=jax.ShapeDtypeStruct((t, D_MODEL), F32),
        scratch_shapes=[pltpu.VMEM((tm + FFN_HALO, D_MODEL), BF16)]
        + [pltpu.VMEM((tm // FFN_ROW_SPLITS + FFN_HALO, tn), F32) for _ in range(2 * FFN_ROW_SPLITS)],
        compiler_params=_compiler_params(("arbitrary", "arbitrary")),
        name="conv_ffn",
    )(h, h, norm_g, w_up, w_up, conv_w, conv_w, conv_b, conv_b, w_down, final_g)


def kernel(x, mem, mix_norm_g, mem_norm_g, w_in, gate_b, conv_w, conv_b, conv_ln_g, conv_ln_b, w_conv_out,
           rel_bias, w_attn_out, w_mem_kv, w_mem_out, w_o, ffn_norm_g, w_up, ffn_conv_w, ffn_conv_b, w_down,
           final_norm_g):
    batch, seq, d = x.shape
    n_mem = mem.shape[1]
    assert (d, n_mem) == (D_MODEL, N_MEM) and seq % ATTN_BLOCK == 0

    def row(p):
        return p.reshape(p.shape[0], 1, p.shape[1])

    w_in_b, w_conv_out_b, w_attn_out_b = w_in.astype(BF16), w_conv_out.astype(BF16), w_attn_out.astype(BF16)
    w_mem_kv_b, w_mem_out_b, w_o_b = w_mem_kv.astype(BF16), w_mem_out.astype(BF16), w_o.astype(BF16)
    w_up_b, w_down_b = w_up.astype(BF16), w_down.astype(BF16)
    mix_g, mem_g, ffn_g = row(mix_norm_g), row(mem_norm_g), row(ffn_norm_g)
    gate_b3, conv_b3, ln_g3, ln_b3, ffn_cb3 = row(gate_b), row(conv_b), row(conv_ln_g), row(conv_ln_b), row(ffn_conv_b)
    final_g = final_norm_g.reshape(1, D_MODEL)
    bias_table = _attn_bias_table(rel_bias)

    h = x.reshape(batch * seq, d)
    mem2 = mem.reshape(batch * n_mem, d)
    for layer in range(DEPTH):
        proj = _norm_matmul(h, mix_g, w_in_b, layer, tm=1024, tn=1024)
        kv = _norm_matmul(mem2, mem_g, w_mem_kv_b, layer, tm=batch * n_mem, tn=1024)
        c = _conv_branch(proj, conv_w, conv_b3, ln_g3, ln_b3, layer, tm=512, seq=seq)
        a = _chunk_attention(proj, bias_table, layer, seq=seq, batch=batch)
        m = _mem_attention(proj, kv, tm=512, seq=seq, batch=batch)
        h = _merge(h, c, a, m, proj, gate_b3, w_conv_out_b, w_attn_out_b, w_mem_out_b, w_o_b, layer, tm=1024, tn=512)
        h = _ffn(h, ffn_g, w_up_b, ffn_conv_w, ffn_cb3, w_down_b, final_g, layer, tm=1024, tn=512, seq=seq,
                 final_norm=(layer == DEPTH - 1))
    return h.reshape(batch, seq, d)
```

```python
import functools

import jax
import jax.numpy as jnp
import numpy as np
from jax import lax
from jax.experimental import pallas as pl
from jax.experimental.pallas import tpu as pltpu

D_MODEL = 2048
DEPTH = 4
CHUNK = 64
N_MEM = 256
N_BRANCH = 3
D_CONV = D_MODEL // 2
CONV_WIDTH = 31
ATTN_HEAD_DIM = 64
D_ATTN = D_MODEL // 2
N_ATTN_HEADS = D_ATTN // ATTN_HEAD_DIM
LEFT_CHUNKS = 8
BAND = LEFT_CHUNKS + 1
MAX_REL = 256
N_MEM_HEADS = 4
D_MEM = D_MODEL // 2
MEM_HEAD_DIM = D_MEM // N_MEM_HEADS
D_FF = ((8 * D_MODEL // 3 + 255) // 256) * 256
FFN_CONV_WIDTH = 3
D_IN = 2 * D_CONV + 3 * D_ATTN + D_MEM + N_BRANCH * D_MODEL
EPS = 1e-6
NEG_INF = -1e30

COL_GLU_A = 0
COL_GLU_G = D_CONV
COL_Q = 2 * D_CONV
COL_K = COL_Q + D_ATTN
COL_V = COL_K + D_ATTN
COL_QMEM = COL_V + D_ATTN
COL_GATES = COL_QMEM + D_MEM

V7X_LANES = 128
V7X_SCOPED_VMEM_BYTES = 60000 * 1024
BF16_SUBLANES = 16

F32 = jnp.float32
BF16 = jnp.bfloat16


def _compiler_params(semantics):
    return pltpu.CompilerParams(dimension_semantics=semantics, vmem_limit_bytes=V7X_SCOPED_VMEM_BYTES)


def _rms(x, g):
    ms = jnp.mean(x * x, axis=-1, keepdims=True)
    return (x * lax.rsqrt(ms + EPS)) * g


def _norm_matmul_kernel(x_ref, g_ref, w_ref, o_ref, xn_ref):
    @pl.when(pl.program_id(1) == 0)
    def _():
        xn_ref[...] = _rms(x_ref[...], g_ref[...]).astype(BF16)

    o_ref[...] = jnp.dot(xn_ref[...], w_ref[...], preferred_element_type=F32).astype(o_ref.dtype)


def _norm_matmul(x, g, w, layer, *, tm, tn):
    t, d = x.shape
    n = w.shape[-1]
    return pl.pallas_call(
        _norm_matmul_kernel,
        grid=(t // tm, n // tn),
        in_specs=[
            pl.BlockSpec((tm, d), lambda i, j: (i, 0)),
            pl.BlockSpec((None, 1, d), lambda i, j: (layer, 0, 0)),
            pl.BlockSpec((None, d, tn), lambda i, j: (layer, 0, j)),
        ],
        out_specs=pl.BlockSpec((tm, tn), lambda i, j: (i, j)),
        out_shape=jax.ShapeDtypeStruct((t, n), BF16),
        scratch_shapes=[pltpu.VMEM((tm, d), BF16)],
        compiler_params=_compiler_params(("arbitrary", "arbitrary")),
        name="norm_matmul",
    )(x, g, w)


F32_SUBLANES = 8
CONV_HALO = 32
CONV_ROWS = 64
CONV_BASE = CONV_HALO - (CONV_WIDTH - 1)
CONV_SHIFT_ROWS = CONV_HALO - F32_SUBLANES


def _conv_kernel(a_ref, gt_ref, ah_ref, gh_ref, cw_ref, cb_ref, lg_ref, lb_ref, o_ref, hbuf_ref, hs_ref, y_ref,
                 *, tiles_per_seq):
    tm = a_ref.shape[0]
    first = (pl.program_id(0) % tiles_per_seq) == 0

    def glu(a, g):
        return a.astype(F32) * jax.nn.sigmoid(g.astype(F32))

    halo = glu(ah_ref[...], gh_ref[...])
    hbuf_ref[0:CONV_HALO, :] = jnp.where(first, 0.0, halo)
    hbuf_ref[CONV_HALO:, :] = glu(a_ref[...], gt_ref[...])

    for s in range(1, F32_SUBLANES):
        hs_ref[s - 1] = hbuf_ref[pl.ds(s, tm + CONV_SHIFT_ROWS), :]

    n_lane_blocks = D_CONV // V7X_LANES

    def step(i, carry):
        r0 = pl.multiple_of((i // n_lane_blocks) * CONV_ROWS, CONV_ROWS)
        lanes = pl.ds(pl.multiple_of((i % n_lane_blocks) * V7X_LANES, V7X_LANES), V7X_LANES)
        acc = jnp.broadcast_to(cb_ref[:, lanes], (CONV_ROWS, V7X_LANES))
        for s in range(F32_SUBLANES):
            taps = [k for k in range(CONV_WIDTH) if (CONV_BASE + k) % F32_SUBLANES == s]
            span = CONV_ROWS + max(CONV_BASE + k - s for k in taps)
            if s == 0:
                xs = hbuf_ref[pl.ds(r0, span), lanes]
            else:
                xs = hs_ref[s - 1, pl.ds(r0, span), lanes]
            for k in taps:
                a = CONV_BASE + k - s
                acc = acc + cw_ref[k:k + 1, lanes] * xs[a:a + CONV_ROWS, :]
        y_ref[pl.ds(r0, CONV_ROWS), lanes] = acc
        return carry

    lax.fori_loop(0, (tm // CONV_ROWS) * n_lane_blocks, step, 0, unroll=4)

    y = y_ref[...]
    mu = jnp.mean(y, axis=-1, keepdims=True)
    yc = y - mu
    var = jnp.mean(yc * yc, axis=-1, keepdims=True)
    z = yc * lax.rsqrt(var + EPS) * lg_ref[...] + lb_ref[...]
    o_ref[...] = (z * jax.nn.sigmoid(z)).astype(o_ref.dtype)


def _conv_branch(proj, conv_w, conv_b, ln_g, ln_b, layer, *, tm, seq):
    t = proj.shape[0]
    halo_blocks = tm // CONV_HALO
    a_blk = COL_GLU_A // D_CONV
    g_blk = COL_GLU_G // D_CONV

    def halo_idx(col):
        return lambda i: (jnp.maximum(i * halo_blocks - 1, 0), col)

    return pl.pallas_call(
        functools.partial(_conv_kernel, tiles_per_seq=seq // tm),
        grid=(t // tm,),
        in_specs=[
            pl.BlockSpec((tm, D_CONV), lambda i: (i, a_blk)),
            pl.BlockSpec((tm, D_CONV), lambda i: (i, g_blk)),
            pl.BlockSpec((CONV_HALO, D_CONV), halo_idx(a_blk)),
            pl.BlockSpec((CONV_HALO, D_CONV), halo_idx(g_blk)),
            pl.BlockSpec((None, CONV_WIDTH, D_CONV), lambda i: (layer, 0, 0)),
            pl.BlockSpec((None, 1, D_CONV), lambda i: (layer, 0, 0)),
            pl.BlockSpec((None, 1, D_CONV), lambda i: (layer, 0, 0)),
            pl.BlockSpec((None, 1, D_CONV), lambda i: (layer, 0, 0)),
        ],
        out_specs=pl.BlockSpec((tm, D_CONV), lambda i: (i, 0)),
        out_shape=jax.ShapeDtypeStruct((t, D_CONV), BF16),
        scratch_shapes=[
            pltpu.VMEM((tm + CONV_HALO, D_CONV), F32),
            pltpu.VMEM((F32_SUBLANES - 1, tm + CONV_SHIFT_ROWS, D_CONV), F32),
            pltpu.VMEM((tm, D_CONV), F32),
        ],
        compiler_params=_compiler_params(("arbitrary",)),
        name="conv_branch",
    )(proj, proj, proj, proj, conv_w, conv_b, ln_g, ln_b)


ATTN_BLOCK = LEFT_CHUNKS * CHUNK
ATTN_SUB = 2 * CHUNK
ATTN_WIN = ATTN_SUB + LEFT_CHUNKS * CHUNK
HEADS_PER_STEP = V7X_LANES // ATTN_HEAD_DIM


def _attn_bias_table(rel_bias):
    extra = ATTN_SUB - CHUNK
    width = ATTN_WIN + extra
    period = CHUNK + width - 1
    m = np.arange(period)
    m_signed = np.where(m < width, m, m - period)
    dist = LEFT_CHUNKS * CHUNK + extra - m_signed
    gp = rel_bias.astype(F32)[:, :, np.clip(dist, -MAX_REL, MAX_REL) + MAX_REL]
    flat = jnp.tile(gp, (1, 1, CHUNK))[:, :, :CHUNK * (period - 1)]
    base = flat.reshape(flat.shape[0], flat.shape[1], CHUNK, period - 1)
    km = np.arange(ATTN_WIN)[None, :] - np.arange(ATTN_SUB // CHUNK)[:, None] * CHUNK
    in_band = (km >= 0) & (km < BAND * CHUNK)
    chunks = [jnp.where(in_band[cq][None, None, None, :],
                        base[:, :, :, extra - cq * CHUNK:extra - cq * CHUNK + ATTN_WIN], NEG_INF)
              for cq in range(ATTN_SUB // CHUNK)]
    table = jnp.concatenate(chunks, axis=2)
    return table.reshape(table.shape[0], N_ATTN_HEADS // HEADS_PER_STEP, HEADS_PER_STEP * ATTN_SUB, ATTN_WIN)


def _attn_kernel(q_ref, km_ref, kh_ref, vm_ref, vh_ref, bias_ref, o_ref, kbuf_ref, vbuf_ref):
    first = (pl.program_id(2) == 0)
    kbuf_ref[0:ATTN_BLOCK, :] = kh_ref[...]
    kbuf_ref[ATTN_BLOCK:, :] = km_ref[...]
    vbuf_ref[0:ATTN_BLOCK, :] = vh_ref[...]
    vbuf_ref[ATTN_BLOCK:, :] = vm_ref[...]

    lane = lax.broadcasted_iota(jnp.int32, (1, V7X_LANES), 1)
    col = lax.broadcasted_iota(jnp.int32, (1, ATTN_WIN), 1)
    scale = ATTN_HEAD_DIM ** -0.5

    n_sub = ATTN_BLOCK // ATTN_SUB
    chains = [(j, h) for j in range(n_sub) for h in range(HEADS_PER_STEP)]

    def window(j):
        return slice(j * ATTN_SUB, j * ATTN_SUB + ATTN_WIN)

    def compute(mask_left):
        scores = []
        for j, h in chains:
            q = q_ref[j * ATTN_SUB:(j + 1) * ATTN_SUB, :] * scale
            qh = jnp.where((lane >= h * ATTN_HEAD_DIM) & (lane < (h + 1) * ATTN_HEAD_DIM), q, jnp.zeros_like(q))
            scores.append(lax.dot_general(qh, kbuf_ref[window(j), :], (((1,), (1,)), ((), ())),
                                          preferred_element_type=F32))
        probs, sums = [], []
        for (j, h), s in zip(chains, scores):
            s = s + bias_ref[h * ATTN_SUB:(h + 1) * ATTN_SUB, :]
            if mask_left:
                s = jnp.where(col < ATTN_BLOCK - j * ATTN_SUB, NEG_INF, s)
            p = jnp.exp(s - jnp.max(s, axis=-1, keepdims=True))
            sums.append(jnp.sum(p, axis=-1, keepdims=True))
            probs.append(p.astype(BF16))
        outs = {}
        for (j, h), p, l in zip(chains, probs, sums):
            outs[j, h] = jnp.dot(p, vbuf_ref[window(j), :], preferred_element_type=F32) / l
        for j in range(n_sub):
            out = outs[j, 0]
            for h in range(1, HEADS_PER_STEP):
                out = jnp.where(lane >= h * ATTN_HEAD_DIM, outs[j, h], out)
            o_ref[j * ATTN_SUB:(j + 1) * ATTN_SUB, :] = out.astype(o_ref.dtype)

    @pl.when(first)
    def _():
        compute(True)

    @pl.when(jnp.logical_not(first))
    def _():
        compute(False)


def _chunk_attention(proj, bias_table, layer, *, seq, batch):
    t = proj.shape[0]
    bps = seq // ATTN_BLOCK
    q_blk, k_blk, v_blk = COL_Q // V7X_LANES, COL_K // V7X_LANES, COL_V // V7X_LANES

    def main_idx(col0):
        return lambda hp, b, i: (b * bps + i, col0 + hp)

    def halo_idx(col0):
        return lambda hp, b, i: (b * bps + jnp.maximum(i - 1, 0), col0 + hp)

    blk = (ATTN_BLOCK, V7X_LANES)
    return pl.pallas_call(
        _attn_kernel,
        grid=(N_ATTN_HEADS // HEADS_PER_STEP, batch, bps),
        in_specs=[
            pl.BlockSpec(blk, main_idx(q_blk)),
            pl.BlockSpec(blk, main_idx(k_blk)),
            pl.BlockSpec(blk, halo_idx(k_blk)),
            pl.BlockSpec(blk, main_idx(v_blk)),
            pl.BlockSpec(blk, halo_idx(v_blk)),
            pl.BlockSpec((None, None, HEADS_PER_STEP * ATTN_SUB, ATTN_WIN), lambda hp, b, i: (layer, hp, 0, 0)),
        ],
        out_specs=pl.BlockSpec(blk, lambda hp, b, i: (b * bps + i, hp)),
        out_shape=jax.ShapeDtypeStruct((t, D_ATTN), BF16),
        scratch_shapes=[pltpu.VMEM((2 * ATTN_BLOCK, V7X_LANES), BF16), pltpu.VMEM((2 * ATTN_BLOCK, V7X_LANES), BF16)],
        compiler_params=_compiler_params(("arbitrary", "arbitrary", "arbitrary")),
        name="chunk_attention",
    )(proj, proj, proj, proj, proj, bias_table)


def _mem_attn_kernel(q_ref, kv_ref, o_ref):
    scale = MEM_HEAD_DIM ** -0.5
    for h in range(N_MEM_HEADS):
        lanes = slice(h * MEM_HEAD_DIM, (h + 1) * MEM_HEAD_DIM)
        q = q_ref[:, lanes] * scale
        k = kv_ref[:, lanes]
        v = kv_ref[:, D_MEM + h * MEM_HEAD_DIM:D_MEM + (h + 1) * MEM_HEAD_DIM]
        s = lax.dot_general(q, k, (((1,), (1,)), ((), ())), preferred_element_type=F32)
        m = jnp.max(s, axis=-1, keepdims=True)
        p = jnp.exp(s - m)
        l = jnp.sum(p, axis=-1, keepdims=True)
        o = jnp.dot(p.astype(BF16), v, preferred_element_type=F32)
        o_ref[:, lanes] = (o / l).astype(o_ref.dtype)


def _mem_attention(proj, kv, *, tm, seq, batch):
    t = proj.shape[0]
    tiles = seq // tm
    return pl.pallas_call(
        _mem_attn_kernel,
        grid=(batch, tiles),
        in_specs=[
            pl.BlockSpec((tm, D_MEM), lambda b, i: (b * tiles + i, COL_QMEM // D_MEM)),
            pl.BlockSpec((N_MEM, 2 * D_MEM), lambda b, i: (b, 0)),
        ],
        out_specs=pl.BlockSpec((tm, D_MEM), lambda b, i: (b * tiles + i, 0)),
        out_shape=jax.ShapeDtypeStruct((t, D_MEM), BF16),
        compiler_params=_compiler_params(("arbitrary", "arbitrary")),
        name="mem_attention",
    )(proj, kv)


def _merge_kernel(h_ref, c_ref, a_ref, m_ref, g0_ref, g1_ref, g2_ref, b0_ref, b1_ref, b2_ref,
                  wc_ref, wa_ref, wm_ref, wo_ref, o_ref):
    j = pl.program_id(1)

    def gated(x_ref, w_ref, g_ref, b_ref):
        y = jnp.dot(x_ref[...], w_ref[...], preferred_element_type=F32)
        return jax.nn.sigmoid(g_ref[...].astype(F32) + b_ref[...]) * y

    @pl.when(j == 0)
    def _():
        o_ref[...] = h_ref[...]

    merged = (gated(c_ref, wc_ref, g0_ref, b0_ref) + gated(a_ref, wa_ref, g1_ref, b1_ref)
              + gated(m_ref, wm_ref, g2_ref, b2_ref))
    o_ref[...] += jnp.dot(merged.astype(BF16), wo_ref[...], preferred_element_type=F32)


def _merge(h, c, a, m, proj, gate_b, w_conv_out, w_attn_out, w_mem_out, w_o, layer, *, tm, tn):
    t = h.shape[0]
    nj = D_MODEL // tn
    gate_blk = COL_GATES // tn

    def gate_idx(branch):
        return lambda i, j: (i, gate_blk + branch * nj + j)

    def bias_idx(branch):
        return lambda i, j: (layer, 0, branch * nj + j)

    row_spec = pl.BlockSpec((tm, D_CONV), lambda i, j: (i, 0))
    w_spec = pl.BlockSpec((None, D_CONV, tn), lambda i, j: (layer, 0, j))
    return pl.pallas_call(
        _merge_kernel,
        grid=(t // tm, nj),
        in_specs=[
            pl.BlockSpec((tm, D_MODEL), lambda i, j: (i, 0)),
            row_spec, row_spec, row_spec,
            pl.BlockSpec((tm, tn), gate_idx(0)),
            pl.BlockSpec((tm, tn), gate_idx(1)),
            pl.BlockSpec((tm, tn), gate_idx(2)),
            pl.BlockSpec((None, 1, tn), bias_idx(0)),
            pl.BlockSpec((None, 1, tn), bias_idx(1)),
            pl.BlockSpec((None, 1, tn), bias_idx(2)),
            w_spec, w_spec, w_spec,
            pl.BlockSpec((None, tn, D_MODEL), lambda i, j: (layer, j, 0)),
        ],
        out_specs=pl.BlockSpec((tm, D_MODEL), lambda i, j: (i, 0), pipeline_mode=pl.Buffered(1)),
        out_shape=jax.ShapeDtypeStruct((t, D_MODEL), F32),
        compiler_params=_compiler_params(("arbitrary", "arbitrary")),
        name="merge",
    )(h, c, a, m, proj, proj, proj, gate_b, gate_b, gate_b, w_conv_out, w_attn_out, w_mem_out, w_o)


FFN_HALO = BF16_SUBLANES
FFN_ROW_SPLITS = 2


def _ffn_kernel(h_ref, halo_ref, g_ref, wv_ref, wg_ref, cwv_ref, cwg_ref, cbv_ref, cbg_ref, wd_ref, fg_ref,
                o_ref, hn_ref, *u_refs, tiles_per_seq, final_norm):
    tm = h_ref.shape[0]
    rows = tm // FFN_ROW_SPLITS
    i, j = pl.program_id(0), pl.program_id(1)

    @pl.when(j == 0)
    def _():
        first = (i % tiles_per_seq) == 0
        halo = _rms(halo_ref[...], g_ref[...])
        hn_ref[0:FFN_HALO, :] = jnp.where(first, 0.0, halo).astype(BF16)
        x = h_ref[...]
        o_ref[...] = x
        hn_ref[FFN_HALO:, :] = _rms(x, g_ref[...]).astype(BF16)

    def up(s):
        hn = hn_ref[s * rows:(s + 1) * rows + FFN_HALO, :]
        u_refs[2 * s][...] = jnp.dot(hn, wv_ref[...], preferred_element_type=F32)
        u_refs[2 * s + 1][...] = jnp.dot(hn, wg_ref[...], preferred_element_type=F32)

    def dwconv(u_ref, cw_ref, cb_ref):
        acc = cb_ref[...]
        for k in range(FFN_CONV_WIDTH):
            acc = acc + cw_ref[k:k + 1, :] * u_ref[pl.ds(FFN_HALO - (FFN_CONV_WIDTH - 1) + k, rows), :]
        return acc

    def down(s):
        val = dwconv(u_refs[2 * s], cwv_ref, cbv_ref)
        gt = dwconv(u_refs[2 * s + 1], cwg_ref, cbg_ref)
        act = (gt * jax.nn.sigmoid(gt) * val).astype(BF16)
        o_ref[s * rows:(s + 1) * rows, :] += jnp.dot(act, wd_ref[...], preferred_element_type=F32)

    up(0)
    for s in range(FFN_ROW_SPLITS):
        if s + 1 < FFN_ROW_SPLITS:
            up(s + 1)
        down(s)

    if final_norm:
        @pl.when(j == pl.num_programs(1) - 1)
        def _():
            o_ref[...] = _rms(o_ref[...], fg_ref[...])


def _ffn(h, norm_g, w_up, conv_w, conv_b, w_down, final_g, layer, *, tm, tn, seq, final_norm):
    t = h.shape[0]
    nj = D_FF // tn
    halo_blocks = tm // FFN_HALO
    return pl.pallas_call(
        functools.partial(_ffn_kernel, tiles_per_seq=seq // tm, final_norm=final_norm),
        grid=(t // tm, nj),
        in_specs=[
            pl.BlockSpec((tm, D_MODEL), lambda i, j: (i, 0)),
            pl.BlockSpec((FFN_HALO, D_MODEL), lambda i, j: (jnp.maximum(i * halo_blocks - 1, 0), 0)),
            pl.BlockSpec((None, 1, D_MODEL), lambda i, j: (layer, 0, 0)),
            pl.BlockSpec((None, D_MODEL, tn), lambda i, j: (layer, 0, j)),
            pl.BlockSpec((None, D_MODEL, tn), lambda i, j: (layer, 0, nj + j)),
            pl.BlockSpec((None, FFN_CONV_WIDTH, tn), lambda i, j: (layer, 0, j)),
            pl.BlockSpec((None, FFN_CONV_WIDTH, tn), lambda i, j: (layer, 0, nj + j)),
            pl.BlockSpec((None, 1, tn), lambda i, j: (layer, 0, j)),
            pl.BlockSpec((None, 1, tn), lambda i, j: (layer, 0, nj + j)),
            pl.BlockSpec((None, tn, D_MODEL), lambda i, j: (layer, j, 0)),
            pl.BlockSpec((1, D_MODEL), lambda i, j: (0, 0)),
        ],
        out_specs=pl.BlockSpec((tm, D_MODEL), lambda i, j: (i, 0)),
        out_shape=jax.ShapeDtypeStruct((t, D_MODEL), F32),
        scratch_shapes=[pltpu.VMEM((tm + FFN_HALO, D_MODEL), BF16)]
        + [pltpu.VMEM((tm // FFN_ROW_SPLITS + FFN_HALO, tn), F32) for _ in range(2 * FFN_ROW_SPLITS)],
        compiler_params=_compiler_params(("arbitrary", "arbitrary")),
        name="conv_ffn",
    )(h, h, norm_g, w_up, w_up, conv_w, conv_w, conv_b, conv_b, w_down, final_g)


def kernel(x, mem, mix_norm_g, mem_norm_g, w_in, gate_b, conv_w, conv_b, conv_ln_g, conv_ln_b, w_conv_out,
           rel_bias, w_attn_out, w_mem_kv, w_mem_out, w_o, ffn_norm_g, w_up, ffn_conv_w, ffn_conv_b, w_down,
           final_norm_g):
    batch, seq, d = x.shape
    n_mem = mem.shape[1]
    assert (d, n_mem) == (D_MODEL, N_MEM) and seq % ATTN_BLOCK == 0

    def row(p):
        return p.reshape(p.shape[0], 1, p.shape[1])

    w_in_b, w_conv_out_b, w_attn_out_b = w_in.astype(BF16), w_conv_out.astype(BF16), w_attn_out.astype(BF16)
    w_mem_kv_b, w_mem_out_b, w_o_b = w_mem_kv.astype(BF16), w_mem_out.astype(BF16), w_o.astype(BF16)
    w_up_b, w_down_b = w_up.astype(BF16), w_down.astype(BF16)
    mix_g, mem_g, ffn_g = row(mix_norm_g), row(mem_norm_g), row(ffn_norm_g)
    gate_b3, conv_b3, ln_g3, ln_b3, ffn_cb3 = row(gate_b), row(conv_b), row(conv_ln_g), row(conv_ln_b), row(ffn_conv_b)
    final_g = final_norm_g.reshape(1, D_MODEL)
    bias_table = _attn_bias_table(rel_bias)

    h = x.reshape(batch * seq, d)
    mem2 = mem.reshape(batch * n_mem, d)
    for layer in range(DEPTH):
        proj = _norm_matmul(h, mix_g, w_in_b, layer, tm=1024, tn=1024)
        kv = _norm_matmul(mem2, mem_g, w_mem_kv_b, layer, tm=batch * n_mem, tn=1024)
        c = _conv_branch(proj, conv_w, conv_b3, ln_g3, ln_b3, layer, tm=512, seq=seq)
        a = _chunk_attention(proj, bias_table, layer, seq=seq, batch=batch)
        m = _mem_attention(proj, kv, tm=512, seq=seq, batch=batch)
        h = _merge(h, c, a, m, proj, gate_b3, w_conv_out_b, w_attn_out_b, w_mem_out_b, w_o_b, layer, tm=1024, tn=512)
        h = _ffn(h, ffn_g, w_up_b, ffn_conv_w, ffn_cb3, w_down_b, final_g, layer, tm=1024, tn=512, seq=seq,
                 final_norm=(layer == DEPTH - 1))
    return h.reshape(batch, seq, d)
```

```python
import functools

import jax
import jax.numpy as jnp
import numpy as np
from jax import lax
from jax.experimental import pallas as pl
from jax.experimental.pallas import tpu as pltpu

D_MODEL = 2048
DEPTH = 4
CHUNK = 64
N_MEM = 256
N_BRANCH = 3
D_CONV = D_MODEL // 2
CONV_WIDTH = 31
ATTN_HEAD_DIM = 64
D_ATTN = D_MODEL // 2
N_ATTN_HEADS = D_ATTN // ATTN_HEAD_DIM
LEFT_CHUNKS = 8
BAND = LEFT_CHUNKS + 1
MAX_REL = 256
N_MEM_HEADS = 4
D_MEM = D_MODEL // 2
MEM_HEAD_DIM = D_MEM // N_MEM_HEADS
D_FF = ((8 * D_MODEL // 3 + 255) // 256) * 256
FFN_CONV_WIDTH = 3
D_IN = 2 * D_CONV + 3 * D_ATTN + D_MEM + N_BRANCH * D_MODEL
EPS = 1e-6
NEG_INF = -1e30

COL_GLU_A = 0
COL_GLU_G = D_CONV
COL_Q = 2 * D_CONV
COL_K = COL_Q + D_ATTN
COL_V = COL_K + D_ATTN
COL_QMEM = COL_V + D_ATTN
COL_GATES = COL_QMEM + D_MEM

V7X_LANES = 128
V7X_SCOPED_VMEM_BYTES = 60000 * 1024
BF16_SUBLANES = 16

F32 = jnp.float32
BF16 = jnp.bfloat16


def _compiler_params(semantics):
    return pltpu.CompilerParams(dimension_semantics=semantics, vmem_limit_bytes=V7X_SCOPED_VMEM_BYTES)


def _rms(x, g):
    ms = jnp.mean(x * x, axis=-1, keepdims=True)
    return (x * lax.rsqrt(ms + EPS)) * g


def _norm_matmul_kernel(x_ref, g_ref, w_ref, o_ref, xn_ref):
    @pl.when(pl.program_id(1) == 0)
    def _():
        xn_ref[...] = _rms(x_ref[...], g_ref[...]).astype(BF16)

    o_ref[...] = jnp.dot(xn_ref[...], w_ref[...], preferred_element_type=F32).astype(o_ref.dtype)


def _norm_matmul(x, g, w, layer, *, tm, tn):
    t, d = x.shape
    n = w.shape[-1]
    return pl.pallas_call(
        _norm_matmul_kernel,
        grid=(t // tm, n // tn),
        in_specs=[
            pl.BlockSpec((tm, d), lambda i, j: (i, 0)),
            pl.BlockSpec((None, 1, d), lambda i, j: (layer, 0, 0)),
            pl.BlockSpec((None, d, tn), lambda i, j: (layer, 0, j)),
        ],
        out_specs=pl.BlockSpec((tm, tn), lambda i, j: (i, j)),
        out_shape=jax.ShapeDtypeStruct((t, n), BF16),
        scratch_shapes=[pltpu.VMEM((tm, d), BF16)],
        compiler_params=_compiler_params(("arbitrary", "arbitrary")),
        name="norm_matmul",
    )(x, g, w)


F32_SUBLANES = 8
CONV_HALO = 32
CONV_ROWS = 64
CONV_BASE = CONV_HALO - (CONV_WIDTH - 1)
CONV_SHIFT_ROWS = CONV_HALO - F32_SUBLANES


def _conv_kernel(a_ref, gt_ref, ah_ref, gh_ref, cw_ref, cb_ref, lg_ref, lb_ref, o_ref, hbuf_ref, hs_ref, y_ref,
                 *, tiles_per_seq):
    tm = a_ref.shape[0]
    first = (pl.program_id(0) % tiles_per_seq) == 0

    def glu(a, g):
        return a.astype(F32) * jax.nn.sigmoid(g.astype(F32))

    halo = glu(ah_ref[...], gh_ref[...])
    hbuf_ref[0:CONV_HALO, :] = jnp.where(first, 0.0, halo)
    hbuf_ref[CONV_HALO:, :] = glu(a_ref[...], gt_ref[...])

    for s in range(1, F32_SUBLANES):
        hs_ref[s - 1] = hbuf_ref[pl.ds(s, tm + CONV_SHIFT_ROWS), :]

    n_lane_blocks = D_CONV // V7X_LANES

    def step(i, carry):
        r0 = pl.multiple_of((i // n_lane_blocks) * CONV_ROWS, CONV_ROWS)
        lanes = pl.ds(pl.multiple_of((i % n_lane_blocks) * V7X_LANES, V7X_LANES), V7X_LANES)
        acc = jnp.broadcast_to(cb_ref[:, lanes], (CONV_ROWS, V7X_LANES))
        for s in range(F32_SUBLANES):
            taps = [k for k in range(CONV_WIDTH) if (CONV_BASE + k) % F32_SUBLANES == s]
            span = CONV_ROWS + max(CONV_BASE + k - s for k in taps)
            if s == 0:
                xs = hbuf_ref[pl.ds(r0, span), lanes]
            else:
                xs = hs_ref[s - 1, pl.ds(r0, span), lanes]
            for k in taps:
                a = CONV_BASE + k - s
                acc = acc + cw_ref[k:k + 1, lanes] * xs[a:a + CONV_ROWS, :]
        y_ref[pl.ds(r0, CONV_ROWS), lanes] = acc
        return carry

    lax.fori_loop(0, (tm // CONV_ROWS) * n_lane_blocks, step, 0, unroll=4)

    y = y_ref[...]
    mu = jnp.mean(y, axis=-1, keepdims=True)
    yc = y - mu
    var = jnp.mean(yc * yc, axis=-1, keepdims=True)
    z = yc * lax.rsqrt(var + EPS) * lg_ref[...] + lb_ref[...]
    o_ref[...] = (z * jax.nn.sigmoid(z)).astype(o_ref.dtype)


def _conv_branch(proj, conv_w, conv_b, ln_g, ln_b, layer, *, tm, seq):
    t = proj.shape[0]
    halo_blocks = tm // CONV_HALO
    a_blk = COL_GLU_A // D_CONV
    g_blk = COL_GLU_G // D_CONV

    def halo_idx(col):
        return lambda i: (jnp.maximum(i * halo_blocks - 1, 0), col)

    return pl.pallas_call(
        functools.partial(_conv_kernel, tiles_per_seq=seq // tm),
        grid=(t // tm,),
        in_specs=[
            pl.BlockSpec((tm, D_CONV), lambda i: (i, a_blk)),
            pl.BlockSpec((tm, D_CONV), lambda i: (i, g_blk)),
            pl.BlockSpec((CONV_HALO, D_CONV), halo_idx(a_blk)),
            pl.BlockSpec((CONV_HALO, D_CONV), halo_idx(g_blk)),
            pl.BlockSpec((None, CONV_WIDTH, D_CONV), lambda i: (layer, 0, 0)),
            pl.BlockSpec((None, 1, D_CONV), lambda i: (layer, 0, 0)),
            pl.BlockSpec((None, 1, D_CONV), lambda i: (layer, 0, 0)),
            pl.BlockSpec((None, 1, D_CONV), lambda i: (layer, 0, 0)),
        ],
        out_specs=pl.BlockSpec((tm, D_CONV), lambda i: (i, 0)),
        out_shape=jax.ShapeDtypeStruct((t, D_CONV), BF16),
        scratch_shapes=[
            pltpu.VMEM((tm + CONV_HALO, D_CONV), F32),
            pltpu.VMEM((F32_SUBLANES - 1, tm + CONV_SHIFT_ROWS, D_CONV), F32),
            pltpu.VMEM((tm, D_CONV), F32),
        ],
        compiler_params=_compiler_params(("arbitrary",)),
        name="conv_branch",
    )(proj, proj, proj, proj, conv_w, conv_b, ln_g, ln_b)


ATTN_BLOCK = LEFT_CHUNKS * CHUNK
ATTN_SUB = 2 * CHUNK
ATTN_WIN = ATTN_SUB + LEFT_CHUNKS * CHUNK
HEADS_PER_STEP = V7X_LANES // ATTN_HEAD_DIM


def _attn_bias_table(rel_bias):
    extra = ATTN_SUB - CHUNK
    width = ATTN_WIN + extra
    period = CHUNK + width - 1
    m = np.arange(period)
    m_signed = np.where(m < width, m, m - period)
    dist = LEFT_CHUNKS * CHUNK + extra - m_signed
    gp = rel_bias.astype(F32)[:, :, np.clip(dist, -MAX_REL, MAX_REL) + MAX_REL]
    flat = jnp.tile(gp, (1, 1, CHUNK))[:, :, :CHUNK * (period - 1)]
    base = flat.reshape(flat.shape[0], flat.shape[1], CHUNK, period - 1)
    km = np.arange(ATTN_WIN)[None, :] - np.arange(ATTN_SUB // CHUNK)[:, None] * CHUNK
    in_band = (km >= 0) & (km < BAND * CHUNK)
    chunks = [jnp.where(in_band[cq][None, None, None, :],
                        base[:, :, :, extra - cq * CHUNK:extra - cq * CHUNK + ATTN_WIN], NEG_INF)
              for cq in range(ATTN_SUB // CHUNK)]
    table = jnp.concatenate(chunks, axis=2)
    return table.reshape(table.shape[0], N_ATTN_HEADS // HEADS_PER_STEP, HEADS_PER_STEP * ATTN_SUB, ATTN_WIN)


def _attn_kernel(q_ref, k_ref, v_ref, bias_ref, o_ref, kbuf_ref, vbuf_ref):
    first = (pl.program_id(2) == 0)

    def load_keys(at_sequence_start):
        for buf_ref, src_ref in ((kbuf_ref, k_ref), (vbuf_ref, v_ref)):
            if at_sequence_start:
                buf_ref[0:ATTN_BLOCK, :] = jnp.zeros((ATTN_BLOCK, V7X_LANES), buf_ref.dtype)
            else:
                buf_ref[0:ATTN_BLOCK, :] = buf_ref[ATTN_BLOCK:, :]
            buf_ref[ATTN_BLOCK:, :] = src_ref[...]

    lane = lax.broadcasted_iota(jnp.int32, (1, V7X_LANES), 1)
    col = lax.broadcasted_iota(jnp.int32, (1, ATTN_WIN), 1)
    scale = ATTN_HEAD_DIM ** -0.5

    n_sub = ATTN_BLOCK // ATTN_SUB
    chains = [(j, h) for j in range(n_sub) for h in range(HEADS_PER_STEP)]

    def window(j):
        return slice(j * ATTN_SUB, j * ATTN_SUB + ATTN_WIN)

    def compute(mask_left):
        scores = []
        for j, h in chains:
            q = q_ref[j * ATTN_SUB:(j + 1) * ATTN_SUB, :] * scale
            qh = jnp.where((lane >= h * ATTN_HEAD_DIM) & (lane < (h + 1) * ATTN_HEAD_DIM), q, jnp.zeros_like(q))
            scores.append(lax.dot_general(qh, kbuf_ref[window(j), :], (((1,), (1,)), ((), ())),
                                          preferred_element_type=F32))
        probs, sums = [], []
        for (j, h), s in zip(chains, scores):
            s = s + bias_ref[h * ATTN_SUB:(h + 1) * ATTN_SUB, :]
            if mask_left:
                s = jnp.where(col < ATTN_BLOCK - j * ATTN_SUB, NEG_INF, s)
            p = jnp.exp(s - jnp.max(s, axis=-1, keepdims=True))
            sums.append(jnp.sum(p, axis=-1, keepdims=True))
            probs.append(p.astype(BF16))
        outs = {}
        for (j, h), p, l in zip(chains, probs, sums):
            outs[j, h] = jnp.dot(p, vbuf_ref[window(j), :], preferred_element_type=F32) / l
        for j in range(n_sub):
            out = outs[j, 0]
            for h in range(1, HEADS_PER_STEP):
                out = jnp.where(lane >= h * ATTN_HEAD_DIM, outs[j, h], out)
            o_ref[j * ATTN_SUB:(j + 1) * ATTN_SUB, :] = out.astype(o_ref.dtype)

    @pl.when(first)
    def _():
        load_keys(True)
        compute(True)

    @pl.when(jnp.logical_not(first))
    def _():
        load_keys(False)
        compute(False)


def _chunk_attention(proj, bias_table, layer, *, seq, batch):
    t = proj.shape[0]
    bps = seq // ATTN_BLOCK
    q_blk, k_blk, v_blk = COL_Q // V7X_LANES, COL_K // V7X_LANES, COL_V // V7X_LANES

    def main_idx(col0):
        return lambda hp, b, i: (b * bps + i, col0 + hp)

    blk = (ATTN_BLOCK, V7X_LANES)
    return pl.pallas_call(
        _attn_kernel,
        grid=(N_ATTN_HEADS // HEADS_PER_STEP, batch, bps),
        in_specs=[
            pl.BlockSpec(blk, main_idx(q_blk)),
            pl.BlockSpec(blk, main_idx(k_blk)),
            pl.BlockSpec(blk, main_idx(v_blk)),
            pl.BlockSpec((None, None, HEADS_PER_STEP * ATTN_SUB, ATTN_WIN), lambda hp, b, i: (layer, hp, 0, 0)),
        ],
        out_specs=pl.BlockSpec(blk, lambda hp, b, i: (b * bps + i, hp)),
        out_shape=jax.ShapeDtypeStruct((t, D_ATTN), BF16),
        scratch_shapes=[pltpu.VMEM((2 * ATTN_BLOCK, V7X_LANES), BF16), pltpu.VMEM((2 * ATTN_BLOCK, V7X_LANES), BF16)],
        compiler_params=_compiler_params(("arbitrary", "arbitrary", "arbitrary")),
        name="chunk_attention",
    )(proj, proj, proj, bias_table)


def _mem_attn_kernel(q_ref, kv_ref, o_ref):
    scale = MEM_HEAD_DIM ** -0.5
    for h in range(N_MEM_HEADS):
        lanes = slice(h * MEM_HEAD_DIM, (h + 1) * MEM_HEAD_DIM)
        q = q_ref[:, lanes] * scale
        k = kv_ref[:, lanes]
        v = kv_ref[:, D_MEM + h * MEM_HEAD_DIM:D_MEM + (h + 1) * MEM_HEAD_DIM]
        s = lax.dot_general(q, k, (((1,), (1,)), ((), ())), preferred_element_type=F32)
        m = jnp.max(s, axis=-1, keepdims=True)
        p = jnp.exp(s - m)
        l = jnp.sum(p, axis=-1, keepdims=True)
        o = jnp.dot(p.astype(BF16), v, preferred_element_type=F32)
        o_ref[:, lanes] = (o / l).astype(o_ref.dtype)


def _mem_attention(proj, kv, *, tm, seq, batch):
    t = proj.shape[0]
    tiles = seq // tm
    return pl.pallas_call(
        _mem_attn_kernel,
        grid=(batch, tiles),
        in_specs=[
            pl.BlockSpec((tm, D_MEM), lambda b, i: (b * tiles + i, COL_QMEM // D_MEM)),
            pl.BlockSpec((N_MEM, 2 * D_MEM), lambda b, i: (b, 0)),
        ],
        out_specs=pl.BlockSpec((tm, D_MEM), lambda b, i: (b * tiles + i, 0)),
        out_shape=jax.ShapeDtypeStruct((t, D_MEM), BF16),
        compiler_params=_compiler_params(("arbitrary", "arbitrary")),
        name="mem_attention",
    )(proj, kv)


def _merge_kernel(h_ref, c_ref, a_ref, m_ref, g0_ref, g1_ref, g2_ref, b0_ref, b1_ref, b2_ref,
                  wc_ref, wa_ref, wm_ref, wo_ref, o_ref):
    j = pl.program_id(1)
    last = pl.num_programs(1) - 1

    def gated(x_ref, w_ref, g_ref, b_ref):
        y = jnp.dot(x_ref[...], w_ref[...], preferred_element_type=F32)
        return jax.nn.sigmoid(g_ref[...].astype(F32) + b_ref[...]) * y

    def contribution():
        merged = (gated(c_ref, wc_ref, g0_ref, b0_ref) + gated(a_ref, wa_ref, g1_ref, b1_ref)
                  + gated(m_ref, wm_ref, g2_ref, b2_ref))
        return jnp.dot(merged.astype(BF16), wo_ref[...], preferred_element_type=F32)

    @pl.when(j == 0)
    def _():
        o_ref[...] = contribution()

    @pl.when((j > 0) & (j < last))
    def _():
        o_ref[...] += contribution()

    @pl.when(j == last)
    def _():
        o_ref[...] += contribution() + h_ref[...]


def _merge(h, c, a, m, proj, gate_b, w_conv_out, w_attn_out, w_mem_out, w_o, layer, *, tm, tn):
    t = h.shape[0]
    nj = D_MODEL // tn
    gate_blk = COL_GATES // tn

    def gate_idx(branch):
        return lambda i, j: (i, gate_blk + branch * nj + j)

    def bias_idx(branch):
        return lambda i, j: (layer, 0, branch * nj + j)

    row_spec = pl.BlockSpec((tm, D_CONV), lambda i, j: (i, 0))
    w_spec = pl.BlockSpec((None, D_CONV, tn), lambda i, j: (layer, 0, j))
    return pl.pallas_call(
        _merge_kernel,
        grid=(t // tm, nj),
        in_specs=[
            pl.BlockSpec((tm, D_MODEL), lambda i, j: (jnp.where(j == nj - 1, i, jnp.maximum(i - 1, 0)), 0)),
            row_spec, row_spec, row_spec,
            pl.BlockSpec((tm, tn), gate_idx(0)),
            pl.BlockSpec((tm, tn), gate_idx(1)),
            pl.BlockSpec((tm, tn), gate_idx(2)),
            pl.BlockSpec((None, 1, tn), bias_idx(0)),
            pl.BlockSpec((None, 1, tn), bias_idx(1)),
            pl.BlockSpec((None, 1, tn), bias_idx(2)),
            w_spec, w_spec, w_spec,
            pl.BlockSpec((None, tn, D_MODEL), lambda i, j: (layer, j, 0)),
        ],
        out_specs=pl.BlockSpec((tm, D_MODEL), lambda i, j: (i, 0), pipeline_mode=pl.Buffered(1)),
        out_shape=jax.ShapeDtypeStruct((t, D_MODEL), F32),
        compiler_params=_compiler_params(("arbitrary", "arbitrary")),
        name="merge",
    )(h, c, a, m, proj, proj, proj, gate_b, gate_b, gate_b, w_conv_out, w_attn_out, w_mem_out, w_o)


FFN_HALO = BF16_SUBLANES
FFN_ROW_SPLITS = 2


def _ffn_kernel(h_ref, halo_ref, g_ref, wv_ref, wg_ref, cwv_ref, cwg_ref, cbv_ref, cbg_ref, wd_ref, fg_ref,
                o_ref, hn_ref, *u_refs, tiles_per_seq, final_norm):
    tm = h_ref.shape[0]
    rows = tm // FFN_ROW_SPLITS
    i, j = pl.program_id(0), pl.program_id(1)

    @pl.when(j == 0)
    def _():
        first = (i % tiles_per_seq) == 0
        halo = _rms(halo_ref[...], g_ref[...])
        hn_ref[0:FFN_HALO, :] = jnp.where(first, 0.0, halo).astype(BF16)
        x = h_ref[...]
        o_ref[...] = x
        hn_ref[FFN_HALO:, :] = _rms(x, g_ref[...]).astype(BF16)

    def up(s):
        hn = hn_ref[s * rows:(s + 1) * rows + FFN_HALO, :]
        u_refs[2 * s][...] = jnp.dot(hn, wv_ref[...], preferred_element_type=F32)
        u_refs[2 * s + 1][...] = jnp.dot(hn, wg_ref[...], preferred_element_type=F32)

    def dwconv(u_ref, cw_ref, cb_ref):
        acc = cb_ref[...]
        for k in range(FFN_CONV_WIDTH):
            acc = acc + cw_ref[k:k + 1, :] * u_ref[pl.ds(FFN_HALO - (FFN_CONV_WIDTH - 1) + k, rows), :]
        return acc

    def down(s):
        val = dwconv(u_refs[2 * s], cwv_ref, cbv_ref)
        gt = dwconv(u_refs[2 * s + 1], cwg_ref, cbg_ref)
        act = (gt * jax.nn.sigmoid(gt) * val).astype(BF16)
        o_ref[s * rows:(s + 1) * rows, :] += jnp.dot(act, wd_ref[...], preferred_element_type=F32)

    up(0)
    for s in range(FFN_ROW_SPLITS):
        if s + 1 < FFN_ROW_SPLITS:
            up(s + 1)
        down(s)

    if final_norm:
        @pl.when(j == pl.num_programs(1) - 1)
        def _():
            o_ref[...] = _rms(o_ref[...], fg_ref[...])


def _ffn(h, norm_g, w_up, conv_w, conv_b, w_down, final_g, layer, *, tm, tn, seq, final_norm):
    t = h.shape[0]
    nj = D_FF // tn
    halo_blocks = tm // FFN_HALO
    return pl.pallas_call(
        functools.partial(_ffn_kernel, tiles_per_seq=seq // tm, final_norm=final_norm),
        grid=(t // tm, nj),
        in_specs=[
            pl.BlockSpec((tm, D_MODEL), lambda i, j: (i, 0)),
            pl.BlockSpec((FFN_HALO, D_MODEL), lambda i, j: (jnp.maximum(i * halo_blocks - 1, 0), 0)),
            pl.BlockSpec((None, 1, D_MODEL), lambda i, j: (layer, 0, 0)),
            pl.BlockSpec((None, D_MODEL, tn), lambda i, j: (layer, 0, j)),
            pl.BlockSpec((None, D_MODEL, tn), lambda i, j: (layer, 0, nj + j)),
            pl.BlockSpec((None, FFN_CONV_WIDTH, tn), lambda i, j: (layer, 0, j)),
            pl.BlockSpec((None, FFN_CONV_WIDTH, tn), lambda i, j: (layer, 0, nj + j)),
            pl.BlockSpec((None, 1, tn), lambda i, j: (layer, 0, j)),
            pl.BlockSpec((None, 1, tn), lambda i, j: (layer, 0, nj + j)),
            pl.BlockSpec((None, tn, D_MODEL), lambda i, j: (layer, j, 0)),
            pl.BlockSpec((1, D_MODEL), lambda i, j: (0, 0)),
        ],
        out_specs=pl.BlockSpec((tm, D_MODEL), lambda i, j: (i, 0)),
        out_shape=jax.ShapeDtypeStruct((t, D_MODEL), F32),
        scratch_shapes=[pltpu.VMEM((tm + FFN_HALO, D_MODEL), BF16)]
        + [pltpu.VMEM((tm // FFN_ROW_SPLITS + FFN_HALO, tn), F32) for _ in range(2 * FFN_ROW_SPLITS)],
        compiler_params=_compiler_params(("arbitrary", "arbitrary")),
        name="conv_ffn",
    )(h, h, norm_g, w_up, w_up, conv_w, conv_w, conv_b, conv_b, w_down, final_g)


def kernel(x, mem, mix_norm_g, mem_norm_g, w_in, gate_b, conv_w, conv_b, conv_ln_g, conv_ln_b, w_conv_out,
           rel_bias, w_attn_out, w_mem_kv, w_mem_out, w_o, ffn_norm_g, w_up, ffn_conv_w, ffn_conv_b, w_down,
           final_norm_g):
    batch, seq, d = x.shape
    n_mem = mem.shape[1]
    assert (d, n_mem) == (D_MODEL, N_MEM) and seq % ATTN_BLOCK == 0

    def row(p):
        return p.reshape(p.shape[0], 1, p.shape[1])

    w_in_b, w_conv_out_b, w_attn_out_b = w_in.astype(BF16), w_conv_out.astype(BF16), w_attn_out.astype(BF16)
    w_mem_kv_b, w_mem_out_b, w_o_b = w_mem_kv.astype(BF16), w_mem_out.astype(BF16), w_o.astype(BF16)
    w_up_b, w_down_b = w_up.astype(BF16), w_down.astype(BF16)
    mix_g, mem_g, ffn_g = row(mix_norm_g), row(mem_norm_g), row(ffn_norm_g)
    gate_b3, conv_b3, ln_g3, ln_b3, ffn_cb3 = row(gate_b), row(conv_b), row(conv_ln_g), row(conv_ln_b), row(ffn_conv_b)
    final_g = final_norm_g.reshape(1, D_MODEL)
    bias_table = _attn_bias_table(rel_bias)

    h = x.reshape(batch * seq, d)
    mem2 = mem.reshape(batch * n_mem, d)
    for layer in range(DEPTH):
        proj = _norm_matmul(h, mix_g, w_in_b, layer, tm=1024, tn=1024)
        kv = _norm_matmul(mem2, mem_g, w_mem_kv_b, layer, tm=batch * n_mem, tn=1024)
        c = _conv_branch(proj, conv_w, conv_b3, ln_g3, ln_b3, layer, tm=512, seq=seq)
        a = _chunk_attention(proj, bias_table, layer, seq=seq, batch=batch)
        m = _mem_attention(proj, kv, tm=512, seq=seq, batch=batch)
        h = _merge(h, c, a, m, proj, gate_b3, w_conv_out_b, w_attn_out_b, w_mem_out_b, w_o_b, layer, tm=1024, tn=512)
        h = _ffn(h, ffn_g, w_up_b, ffn_conv_w, ffn_cb3, w_down_b, final_g, layer, tm=1024, tn=512, seq=seq,
                 final_norm=(layer == DEPTH - 1))
    return h.reshape(batch, seq, d)
```

```python
import functools

import jax
import jax.numpy as jnp
import numpy as np
from jax import lax
from jax.experimental import pallas as pl
from jax.experimental.pallas import tpu as pltpu

D_MODEL = 2048
DEPTH = 4
CHUNK = 64
N_MEM = 256
N_BRANCH = 3
D_CONV = D_MODEL // 2
CONV_WIDTH = 31
ATTN_HEAD_DIM = 64
D_ATTN = D_MODEL // 2
N_ATTN_HEADS = D_ATTN // ATTN_HEAD_DIM
LEFT_CHUNKS = 8
BAND = LEFT_CHUNKS + 1
MAX_REL = 256
N_MEM_HEADS = 4
D_MEM = D_MODEL // 2
MEM_HEAD_DIM = D_MEM // N_MEM_HEADS
D_FF = ((8 * D_MODEL // 3 + 255) // 256) * 256
FFN_CONV_WIDTH = 3
D_IN = 2 * D_CONV + 3 * D_ATTN + D_MEM + N_BRANCH * D_MODEL
EPS = 1e-6
NEG_INF = -1e30

COL_GLU_A = 0
COL_GLU_G = D_CONV
COL_Q = 2 * D_CONV
COL_K = COL_Q + D_ATTN
COL_V = COL_K + D_ATTN
COL_QMEM = COL_V + D_ATTN
COL_GATES = COL_QMEM + D_MEM

V7X_LANES = 128
V7X_SCOPED_VMEM_BYTES = 60000 * 1024
BF16_SUBLANES = 16

F32 = jnp.float32
BF16 = jnp.bfloat16


def _compiler_params(semantics):
    return pltpu.CompilerParams(dimension_semantics=semantics, vmem_limit_bytes=V7X_SCOPED_VMEM_BYTES)


def _rms(x, g):
    ms = jnp.mean(x * x, axis=-1, keepdims=True)
    return (x * lax.rsqrt(ms + EPS)) * g


def _norm_matmul_kernel(x_ref, g_ref, w_ref, o_ref, xn_ref):
    @pl.when(pl.program_id(1) == 0)
    def _():
        xn_ref[...] = _rms(x_ref[...], g_ref[...]).astype(BF16)

    o_ref[...] = jnp.dot(xn_ref[...], w_ref[...], preferred_element_type=F32).astype(o_ref.dtype)


def _norm_matmul(x, g, w, layer, *, tm, tn):
    t, d = x.shape
    n = w.shape[-1]
    return pl.pallas_call(
        _norm_matmul_kernel,
        grid=(t // tm, n // tn),
        in_specs=[
            pl.BlockSpec((tm, d), lambda i, j: (i, 0)),
            pl.BlockSpec((None, 1, d), lambda i, j: (layer, 0, 0)),
            pl.BlockSpec((None, d, tn), lambda i, j: (layer, 0, j)),
        ],
        out_specs=pl.BlockSpec((tm, tn), lambda i, j: (i, j)),
        out_shape=jax.ShapeDtypeStruct((t, n), BF16),
        scratch_shapes=[pltpu.VMEM((tm, d), BF16)],
        compiler_params=_compiler_params(("arbitrary", "arbitrary")),
        name="norm_matmul",
    )(x, g, w)


F32_SUBLANES = 8
CONV_HALO = 32
CONV_ROWS = 64
CONV_BASE = CONV_HALO - (CONV_WIDTH - 1)
CONV_SHIFT_ROWS = CONV_HALO - F32_SUBLANES


def _conv_kernel(a_ref, gt_ref, ah_ref, gh_ref, cw_ref, cb_ref, lg_ref, lb_ref, o_ref, hbuf_ref, hs_ref, y_ref,
                 *, tiles_per_seq):
    tm = a_ref.shape[0]
    first = (pl.program_id(0) % tiles_per_seq) == 0

    def glu(a, g):
        return a.astype(F32) * jax.nn.sigmoid(g.astype(F32))

    halo = glu(ah_ref[...], gh_ref[...])
    hbuf_ref[0:CONV_HALO, :] = jnp.where(first, 0.0, halo)
    hbuf_ref[CONV_HALO:, :] = glu(a_ref[...], gt_ref[...])

    for s in range(1, F32_SUBLANES):
        hs_ref[s - 1] = hbuf_ref[pl.ds(s, tm + CONV_SHIFT_ROWS), :]

    n_lane_blocks = D_CONV // V7X_LANES

    def step(i, carry):
        r0 = pl.multiple_of((i // n_lane_blocks) * CONV_ROWS, CONV_ROWS)
        lanes = pl.ds(pl.multiple_of((i % n_lane_blocks) * V7X_LANES, V7X_LANES), V7X_LANES)
        acc = jnp.broadcast_to(cb_ref[:, lanes], (CONV_ROWS, V7X_LANES))
        for s in range(F32_SUBLANES):
            taps = [k for k in range(CONV_WIDTH) if (CONV_BASE + k) % F32_SUBLANES == s]
            span = CONV_ROWS + max(CONV_BASE + k - s for k in taps)
            if s == 0:
                xs = hbuf_ref[pl.ds(r0, span), lanes]
            else:
                xs = hs_ref[s - 1, pl.ds(r0, span), lanes]
            for k in taps:
                a = CONV_BASE + k - s
                acc = acc + cw_ref[k:k + 1, lanes] * xs[a:a + CONV_ROWS, :]
        y_ref[pl.ds(r0, CONV_ROWS), lanes] = acc
        return carry

    lax.fori_loop(0, (tm // CONV_ROWS) * n_lane_blocks, step, 0, unroll=4)

    y = y_ref[...]
    mu = jnp.mean(y, axis=-1, keepdims=True)
    yc = y - mu
    var = jnp.mean(yc * yc, axis=-1, keepdims=True)
    z = yc * lax.rsqrt(var + EPS) * lg_ref[...] + lb_ref[...]
    o_ref[...] = (z * jax.nn.sigmoid(z)).astype(o_ref.dtype)


def _conv_branch(proj, conv_w, conv_b, ln_g, ln_b, layer, *, tm, seq):
    t = proj.shape[0]
    halo_blocks = tm // CONV_HALO
    a_blk = COL_GLU_A // D_CONV
    g_blk = COL_GLU_G // D_CONV

    def halo_idx(col):
        return lambda i: (jnp.maximum(i * halo_blocks - 1, 0), col)

    return pl.pallas_call(
        functools.partial(_conv_kernel, tiles_per_seq=seq // tm),
        grid=(t // tm,),
        in_specs=[
            pl.BlockSpec((tm, D_CONV), lambda i: (i, a_blk)),
            pl.BlockSpec((tm, D_CONV), lambda i: (i, g_blk)),
            pl.BlockSpec((CONV_HALO, D_CONV), halo_idx(a_blk)),
            pl.BlockSpec((CONV_HALO, D_CONV), halo_idx(g_blk)),
            pl.BlockSpec((None, CONV_WIDTH, D_CONV), lambda i: (layer, 0, 0)),
            pl.BlockSpec((None, 1, D_CONV), lambda i: (layer, 0, 0)),
            pl.BlockSpec((None, 1, D_CONV), lambda i: (layer, 0, 0)),
            pl.BlockSpec((None, 1, D_CONV), lambda i: (layer, 0, 0)),
        ],
        out_specs=pl.BlockSpec((tm, D_CONV), lambda i: (i, 0)),
        out_shape=jax.ShapeDtypeStruct((t, D_CONV), BF16),
        scratch_shapes=[
            pltpu.VMEM((tm + CONV_HALO, D_CONV), F32),
            pltpu.VMEM((F32_SUBLANES - 1, tm + CONV_SHIFT_ROWS, D_CONV), F32),
            pltpu.VMEM((tm, D_CONV), F32),
        ],
        compiler_params=_compiler_params(("arbitrary",)),
        name="conv_branch",
    )(proj, proj, proj, proj, conv_w, conv_b, ln_g, ln_b)


ATTN_BLOCK = LEFT_CHUNKS * CHUNK
ATTN_SUB = 2 * CHUNK
ATTN_WIN = ATTN_SUB + LEFT_CHUNKS * CHUNK
HEADS_PER_STEP = V7X_LANES // ATTN_HEAD_DIM


def _attn_bias_table(rel_bias):
    extra = ATTN_SUB - CHUNK
    width = ATTN_WIN + extra
    period = CHUNK + width - 1
    m = np.arange(period)
    m_signed = np.where(m < width, m, m - period)
    dist = LEFT_CHUNKS * CHUNK + extra - m_signed
    gp = rel_bias.astype(F32)[:, :, np.clip(dist, -MAX_REL, MAX_REL) + MAX_REL]
    flat = jnp.tile(gp, (1, 1, CHUNK))[:, :, :CHUNK * (period - 1)]
    base = flat.reshape(flat.shape[0], flat.shape[1], CHUNK, period - 1)
    km = np.arange(ATTN_WIN)[None, :] - np.arange(ATTN_SUB // CHUNK)[:, None] * CHUNK
    in_band = (km >= 0) & (km < BAND * CHUNK)
    chunks = [jnp.where(in_band[cq][None, None, None, :],
                        base[:, :, :, extra - cq * CHUNK:extra - cq * CHUNK + ATTN_WIN], NEG_INF)
              for cq in range(ATTN_SUB // CHUNK)]
    table = jnp.concatenate(chunks, axis=2)
    return table.reshape(table.shape[0], N_ATTN_HEADS // HEADS_PER_STEP, HEADS_PER_STEP * ATTN_SUB, ATTN_WIN)


def _attn_kernel(q_ref, k_ref, v_ref, bias_ref, o_ref, kbuf_ref, vbuf_ref):
    first = (pl.program_id(2) == 0)

    def load_keys(at_sequence_start):
        for buf_ref, src_ref in ((kbuf_ref, k_ref), (vbuf_ref, v_ref)):
            if at_sequence_start:
                buf_ref[0:ATTN_BLOCK, :] = jnp.zeros((ATTN_BLOCK, V7X_LANES), buf_ref.dtype)
            else:
                buf_ref[0:ATTN_BLOCK, :] = buf_ref[ATTN_BLOCK:, :]
            buf_ref[ATTN_BLOCK:, :] = src_ref[...]

    lane = lax.broadcasted_iota(jnp.int32, (1, V7X_LANES), 1)
    col = lax.broadcasted_iota(jnp.int32, (1, ATTN_WIN), 1)
    scale = ATTN_HEAD_DIM ** -0.5

    n_sub = ATTN_BLOCK // ATTN_SUB
    chains = [(j, h) for j in range(n_sub) for h in range(HEADS_PER_STEP)]

    def window(j):
        return slice(j * ATTN_SUB, j * ATTN_SUB + ATTN_WIN)

    def compute(mask_left):
        scores = []
        for j, h in chains:
            q = q_ref[j * ATTN_SUB:(j + 1) * ATTN_SUB, :] * scale
            qh = jnp.where((lane >= h * ATTN_HEAD_DIM) & (lane < (h + 1) * ATTN_HEAD_DIM), q, jnp.zeros_like(q))
            scores.append(lax.dot_general(qh, kbuf_ref[window(j), :], (((1,), (1,)), ((), ())),
                                          preferred_element_type=F32))
        probs, sums = [], []
        for (j, h), s in zip(chains, scores):
            s = s + bias_ref[h * ATTN_SUB:(h + 1) * ATTN_SUB, :]
            if mask_left:
                s = jnp.where(col < ATTN_BLOCK - j * ATTN_SUB, NEG_INF, s)
            p = jnp.exp(s - jnp.max(s, axis=-1, keepdims=True))
            sums.append(jnp.sum(p, axis=-1, keepdims=True))
            probs.append(p.astype(BF16))
        outs = {}
        for (j, h), p, l in zip(chains, probs, sums):
            outs[j, h] = jnp.dot(p, vbuf_ref[window(j), :], preferred_element_type=F32) / l
        for j in range(n_sub):
            out = outs[j, 0]
            for h in range(1, HEADS_PER_STEP):
                out = jnp.where(lane >= h * ATTN_HEAD_DIM, outs[j, h], out)
            o_ref[j * ATTN_SUB:(j + 1) * ATTN_SUB, :] = out.astype(o_ref.dtype)

    @pl.when(first)
    def _():
        load_keys(True)
        compute(True)

    @pl.when(jnp.logical_not(first))
    def _():
        load_keys(False)
        compute(False)


def _chunk_attention(proj, bias_table, layer, *, seq, batch):
    t = proj.shape[0]
    bps = seq // ATTN_BLOCK
    q_blk, k_blk, v_blk = COL_Q // V7X_LANES, COL_K // V7X_LANES, COL_V // V7X_LANES

    def main_idx(col0):
        return lambda hp, b, i: (b * bps + i, col0 + hp)

    blk = (ATTN_BLOCK, V7X_LANES)
    return pl.pallas_call(
        _attn_kernel,
        grid=(N_ATTN_HEADS // HEADS_PER_STEP, batch, bps),
        in_specs=[
            pl.BlockSpec(blk, main_idx(q_blk)),
            pl.BlockSpec(blk, main_idx(k_blk)),
            pl.BlockSpec(blk, main_idx(v_blk)),
            pl.BlockSpec((None, None, HEADS_PER_STEP * ATTN_SUB, ATTN_WIN), lambda hp, b, i: (layer, hp, 0, 0)),
        ],
        out_specs=pl.BlockSpec(blk, lambda hp, b, i: (b * bps + i, hp)),
        out_shape=jax.ShapeDtypeStruct((t, D_ATTN), BF16),
        scratch_shapes=[pltpu.VMEM((2 * ATTN_BLOCK, V7X_LANES), BF16), pltpu.VMEM((2 * ATTN_BLOCK, V7X_LANES), BF16)],
        compiler_params=_compiler_params(("arbitrary", "arbitrary", "arbitrary")),
        name="chunk_attention",
    )(proj, proj, proj, bias_table)


def _mem_attn_kernel(q_ref, kv_ref, o_ref):
    scale = MEM_HEAD_DIM ** -0.5
    for h in range(N_MEM_HEADS):
        lanes = slice(h * MEM_HEAD_DIM, (h + 1) * MEM_HEAD_DIM)
        q = q_ref[:, lanes] * scale
        k = kv_ref[:, lanes]
        v = kv_ref[:, D_MEM + h * MEM_HEAD_DIM:D_MEM + (h + 1) * MEM_HEAD_DIM]
        s = lax.dot_general(q, k, (((1,), (1,)), ((), ())), preferred_element_type=F32)
        m = jnp.max(s, axis=-1, keepdims=True)
        p = jnp.exp(s - m)
        l = jnp.sum(p, axis=-1, keepdims=True)
        o = jnp.dot(p.astype(BF16), v, preferred_element_type=F32)
        o_ref[:, lanes] = (o / l).astype(o_ref.dtype)


def _mem_attention(proj, kv, *, tm, seq, batch):
    t = proj.shape[0]
    tiles = seq // tm
    return pl.pallas_call(
        _mem_attn_kernel,
        grid=(batch, tiles),
        in_specs=[
            pl.BlockSpec((tm, D_MEM), lambda b, i: (b * tiles + i, COL_QMEM // D_MEM)),
            pl.BlockSpec((N_MEM, 2 * D_MEM), lambda b, i: (b, 0)),
        ],
        out_specs=pl.BlockSpec((tm, D_MEM), lambda b, i: (b * tiles + i, 0)),
        out_shape=jax.ShapeDtypeStruct((t, D_MEM), BF16),
        compiler_params=_compiler_params(("arbitrary", "arbitrary")),
        name="mem_attention",
    )(proj, kv)


def _merge_kernel(h_ref, c_ref, a_ref, m_ref, g0_ref, g1_ref, g2_ref, gb_ref, wb_ref, wo_ref, o_ref):
    j = pl.program_id(1)
    last = pl.num_programs(1) - 1

    def gated(branch, x_ref, g_ref):
        y = jnp.dot(x_ref[...], wb_ref[branch * D_CONV:(branch + 1) * D_CONV, :], preferred_element_type=F32)
        return jax.nn.sigmoid(g_ref[...].astype(F32) + gb_ref[branch:branch + 1, :]) * y

    def contribution():
        merged = gated(0, c_ref, g0_ref) + gated(1, a_ref, g1_ref) + gated(2, m_ref, g2_ref)
        return jnp.dot(merged.astype(BF16), wo_ref[...], preferred_element_type=F32)

    @pl.when(j == 0)
    def _():
        o_ref[...] = contribution()

    @pl.when((j > 0) & (j < last))
    def _():
        o_ref[...] += contribution()

    @pl.when(j == last)
    def _():
        o_ref[...] += contribution() + h_ref[...]


def _merge(h, c, a, m, proj, gate_b, w_branch, w_o, layer, *, tm, tn):
    t = h.shape[0]
    nj = D_MODEL // tn
    gate_blk = COL_GATES // tn

    def gate_idx(branch):
        return lambda i, j: (i, gate_blk + branch * nj + j)

    row_spec = pl.BlockSpec((tm, D_CONV), lambda i, j: (i, 0))
    return pl.pallas_call(
        _merge_kernel,
        grid=(t // tm, nj),
        in_specs=[
            pl.BlockSpec((tm, D_MODEL), lambda i, j: (jnp.where(j == nj - 1, i, jnp.maximum(i - 1, 0)), 0)),
            row_spec, row_spec, row_spec,
            pl.BlockSpec((tm, tn), gate_idx(0)),
            pl.BlockSpec((tm, tn), gate_idx(1)),
            pl.BlockSpec((tm, tn), gate_idx(2)),
            pl.BlockSpec((None, N_BRANCH, tn), lambda i, j: (layer, 0, j)),
            pl.BlockSpec((None, N_BRANCH * D_CONV, tn), lambda i, j: (layer, 0, j)),
            pl.BlockSpec((None, tn, D_MODEL), lambda i, j: (layer, j, 0)),
        ],
        out_specs=pl.BlockSpec((tm, D_MODEL), lambda i, j: (i, 0), pipeline_mode=pl.Buffered(1)),
        out_shape=jax.ShapeDtypeStruct((t, D_MODEL), F32),
        compiler_params=_compiler_params(("arbitrary", "arbitrary")),
        name="merge",
    )(h, c, a, m, proj, proj, proj, gate_b, w_branch, w_o)


FFN_HALO = BF16_SUBLANES
FFN_ROW_SPLITS = 2


def _ffn_kernel(h_ref, halo_ref, g_ref, wv_ref, wg_ref, cwv_ref, cwg_ref, cbv_ref, cbg_ref, wd_ref, fg_ref,
                o_ref, hn_ref, *u_refs, tiles_per_seq, final_norm):
    tm = h_ref.shape[0]
    rows = tm // FFN_ROW_SPLITS
    i, j = pl.program_id(0), pl.program_id(1)

    @pl.when(j == 0)
    def _():
        first = (i % tiles_per_seq) == 0
        halo = _rms(halo_ref[...], g_ref[...])
        hn_ref[0:FFN_HALO, :] = jnp.where(first, 0.0, halo).astype(BF16)
        x = h_ref[...]
        o_ref[...] = x
        hn_ref[FFN_HALO:, :] = _rms(x, g_ref[...]).astype(BF16)

    def up(s):
        hn = hn_ref[s * rows:(s + 1) * rows + FFN_HALO, :]
        u_refs[2 * s][...] = jnp.dot(hn, wv_ref[...], preferred_element_type=F32)
        u_refs[2 * s + 1][...] = jnp.dot(hn, wg_ref[...], preferred_element_type=F32)

    def dwconv(u_ref, cw_ref, cb_ref):
        acc = cb_ref[...]
        for k in range(FFN_CONV_WIDTH):
            acc = acc + cw_ref[k:k + 1, :] * u_ref[pl.ds(FFN_HALO - (FFN_CONV_WIDTH - 1) + k, rows), :]
        return acc

    def down(s):
        val = dwconv(u_refs[2 * s], cwv_ref, cbv_ref)
        gt = dwconv(u_refs[2 * s + 1], cwg_ref, cbg_ref)
        act = (gt * jax.nn.sigmoid(gt) * val).astype(BF16)
        o_ref[s * rows:(s + 1) * rows, :] += jnp.dot(act, wd_ref[...], preferred_element_type=F32)

    up(0)
    for s in range(FFN_ROW_SPLITS):
        if s + 1 < FFN_ROW_SPLITS:
            up(s + 1)
        down(s)

    if final_norm:
        @pl.when(j == pl.num_programs(1) - 1)
        def _():
            o_ref[...] = _rms(o_ref[...], fg_ref[...])


def _ffn(h, norm_g, w_up, conv_w, conv_b, w_down, final_g, layer, *, tm, tn, seq, final_norm):
    t = h.shape[0]
    nj = D_FF // tn
    halo_blocks = tm // FFN_HALO
    return pl.pallas_call(
        functools.partial(_ffn_kernel, tiles_per_seq=seq // tm, final_norm=final_norm),
        grid=(t // tm, nj),
        in_specs=[
            pl.BlockSpec((tm, D_MODEL), lambda i, j: (i, 0)),
            pl.BlockSpec((FFN_HALO, D_MODEL), lambda i, j: (jnp.maximum(i * halo_blocks - 1, 0), 0)),
            pl.BlockSpec((None, 1, D_MODEL), lambda i, j: (layer, 0, 0)),
            pl.BlockSpec((None, D_MODEL, tn), lambda i, j: (layer, 0, j)),
            pl.BlockSpec((None, D_MODEL, tn), lambda i, j: (layer, 0, nj + j)),
            pl.BlockSpec((None, FFN_CONV_WIDTH, tn), lambda i, j: (layer, 0, j)),
            pl.BlockSpec((None, FFN_CONV_WIDTH, tn), lambda i, j: (layer, 0, nj + j)),
            pl.BlockSpec((None, 1, tn), lambda i, j: (layer, 0, j)),
            pl.BlockSpec((None, 1, tn), lambda i, j: (layer, 0, nj + j)),
            pl.BlockSpec((None, tn, D_MODEL), lambda i, j: (layer, j, 0)),
            pl.BlockSpec((1, D_MODEL), lambda i, j: (0, 0)),
        ],
        out_specs=pl.BlockSpec((tm, D_MODEL), lambda i, j: (i, 0)),
        out_shape=jax.ShapeDtypeStruct((t, D_MODEL), F32),
        scratch_shapes=[pltpu.VMEM((tm + FFN_HALO, D_MODEL), BF16)]
        + [pltpu.VMEM((tm // FFN_ROW_SPLITS + FFN_HALO, tn), F32) for _ in range(2 * FFN_ROW_SPLITS)],
        compiler_params=_compiler_params(("arbitrary", "arbitrary")),
        name="conv_ffn",
    )(h, h, norm_g, w_up, w_up, conv_w, conv_w, conv_b, conv_b, w_down, final_g)


def kernel(x, mem, mix_norm_g, mem_norm_g, w_in, gate_b, conv_w, conv_b, conv_ln_g, conv_ln_b, w_conv_out,
           rel_bias, w_attn_out, w_mem_kv, w_mem_out, w_o, ffn_norm_g, w_up, ffn_conv_w, ffn_conv_b, w_down,
           final_norm_g):
    batch, seq, d = x.shape
    n_mem = mem.shape[1]
    assert (d, n_mem) == (D_MODEL, N_MEM) and seq % ATTN_BLOCK == 0

    def row(p):
        return p.reshape(p.shape[0], 1, p.shape[1])

    w_in_b, w_mem_kv_b, w_o_b = w_in.astype(BF16), w_mem_kv.astype(BF16), w_o.astype(BF16)
    w_branch_b = jnp.concatenate([w_conv_out, w_attn_out, w_mem_out], axis=1).astype(BF16)
    w_up_b, w_down_b = w_up.astype(BF16), w_down.astype(BF16)
    mix_g, mem_g, ffn_g = row(mix_norm_g), row(mem_norm_g), row(ffn_norm_g)
    conv_b3, ln_g3, ln_b3, ffn_cb3 = row(conv_b), row(conv_ln_g), row(conv_ln_b), row(ffn_conv_b)
    gate_b3 = gate_b.reshape(gate_b.shape[0], N_BRANCH, D_MODEL)
    final_g = final_norm_g.reshape(1, D_MODEL)
    bias_table = _attn_bias_table(rel_bias)

    h = x.reshape(batch * seq, d)
    mem2 = mem.reshape(batch * n_mem, d)
    for layer in range(DEPTH):
        proj = _norm_matmul(h, mix_g, w_in_b, layer, tm=1024, tn=1024)
        kv = _norm_matmul(mem2, mem_g, w_mem_kv_b, layer, tm=batch * n_mem, tn=1024)
        c = _conv_branch(proj, conv_w, conv_b3, ln_g3, ln_b3, layer, tm=512, seq=seq)
        a = _chunk_attention(proj, bias_table, layer, seq=seq, batch=batch)
        m = _mem_attention(proj, kv, tm=512, seq=seq, batch=batch)
        h = _merge(h, c, a, m, proj, gate_b3, w_branch_b, w_o_b, layer, tm=1024, tn=512)
        h = _ffn(h, ffn_g, w_up_b, ffn_conv_w, ffn_cb3, w_down_b, final_g, layer, tm=1024, tn=512, seq=seq,
                 final_norm=(layer == DEPTH - 1))
    return h.reshape(batch, seq, d)
```
